```python
import math
import jax, jax.numpy as jnp
from jax import lax
import numpy as np

D_MODEL = 2048
BATCH = 4
SEQ = 4096
DEPTH = 4

N_MIXERS = 2
N_ATTN_LAYERS = (DEPTH + 1) // 2
N_CONV_LAYERS = DEPTH // 2
N_DIFF_HEADS = 8
HEAD_DIM = 128
V_HEAD_DIM = 2 * HEAD_DIM
QK_WIDTH = 2 * N_DIFF_HEADS * HEAD_DIM
V_WIDTH = N_DIFF_HEADS * V_HEAD_DIM
ROPE_THETA = 10000.0
Q_BLOCK = 128
CONV_WIDTH = 3
D_FF = (((8 * D_MODEL + 2) // 3 + 255) // 256) * 256
RMS_EPS = 1e-6
SUBLN_EPS = 1e-5

kernel_name = "hybrid_diffattn_shortconv_swiglu"


def rms_norm(x, g, eps):
    xf = x.astype(jnp.float32)
    y = xf * lax.rsqrt(jnp.mean(xf * xf, axis=-1, keepdims=True) + eps)
    return (y * g.astype(jnp.float32)).astype(x.dtype)


def rope_tables(seq_len):
    pos = jnp.arange(seq_len, dtype=jnp.float32)
    inv_freq = 1.0 / (ROPE_THETA ** (jnp.arange(0, HEAD_DIM, 2, dtype=jnp.float32) / HEAD_DIM))
    ang = pos[:, None] * inv_freq[None, :]
    return jnp.cos(ang), jnp.sin(ang)


def apply_rope(t, cos, sin):
    tf = t.astype(jnp.float32)
    t1, t2 = jnp.split(tf, 2, axis=-1)
    c = cos[None, :, None, :]
    s = sin[None, :, None, :]
    out = jnp.concatenate([t1 * c - t2 * s, t2 * c + t1 * s], axis=-1)
    return out.astype(t.dtype)


def diff_attention(h, w_qkv, w_o, lq1, lk1, lq2, lk2, subln_g, lambda_init, cos, sin):
    b, s, _ = h.shape
    qkv = h @ w_qkv
    q, k, v = jnp.split(qkv, [QK_WIDTH, 2 * QK_WIDTH], axis=-1)
    q = apply_rope(q.reshape(b, s, 2 * N_DIFF_HEADS, HEAD_DIM), cos, sin)
    k = apply_rope(k.reshape(b, s, 2 * N_DIFF_HEADS, HEAD_DIM), cos, sin)
    v = v.reshape(b, s, N_DIFF_HEADS, V_HEAD_DIM).astype(jnp.float32)
    lam = (jnp.exp(jnp.sum(lq1.astype(jnp.float32) * lk1.astype(jnp.float32)))
           - jnp.exp(jnp.sum(lq2.astype(jnp.float32) * lk2.astype(jnp.float32)))
           + lambda_init)
    scale = HEAD_DIM ** -0.5
    n_blocks = s // Q_BLOCK
    q_blocks = q.reshape(b, n_blocks, Q_BLOCK, 2 * N_DIFF_HEADS, HEAD_DIM).transpose(1, 0, 2, 3, 4)
    starts = jnp.arange(n_blocks, dtype=jnp.int32) * Q_BLOCK
    k_pos = jnp.arange(s, dtype=jnp.int32)

    def one_block(args):
        qb, start = args
        scores = jnp.einsum('bqhd,bkhd->bhqk', qb, k).astype(jnp.float32) * scale
        q_pos = start + jnp.arange(Q_BLOCK, dtype=jnp.int32)
        causal = k_pos[None, :] <= q_pos[:, None]
        scores = jnp.where(causal[None, None], scores, -jnp.inf)
        p = jax.nn.softmax(scores, axis=-1).reshape(b, N_DIFF_HEADS, 2, Q_BLOCK, s)
        attn = p[:, :, 0] - lam * p[:, :, 1]
        return jnp.einsum('bhqk,bkhe->bqhe', attn, v)

    o = lax.map(one_block, (q_blocks, starts))
    o = o.transpose(1, 0, 2, 3, 4).reshape(b, s, N_DIFF_HEADS, V_HEAD_DIM)
    o = rms_norm(o, subln_g, SUBLN_EPS) * (1.0 - lambda_init)
    return o.reshape(b, s, V_WIDTH).astype(h.dtype) @ w_o


def short_conv_mixer(h, w_bch, conv_w, w_o):
    bch = h @ w_bch
    gate_b, gate_c, u = jnp.split(bch, 3, axis=-1)
    z = gate_c * u
    conv_filter = conv_w[:, None, :].astype(z.dtype)
    zc = lax.conv_general_dilated(
        z, conv_filter, window_strides=(1,), padding=[(CONV_WIDTH - 1, 0)],
        dimension_numbers=('NWC', 'WIO', 'NWC'), feature_group_count=D_MODEL)
    return (gate_b * zc) @ w_o


def swiglu(h, w_gate, w_up, w_down):
    return (jax.nn.silu(h @ w_gate) * (h @ w_up)) @ w_down


def setup_inputs(seed: int = 0) -> dict:
    key = jax.random.key(seed)
    ks = jax.random.split(key, 24)
    f32 = jnp.float32
    def nrm(k, shape, scale):
        return jax.random.normal(k, shape, f32) * scale
    def gain(k, shape):
        return 1.0 + 0.02 * jax.random.normal(k, shape, f32)
    res_scale = (2.0 * DEPTH) ** -0.5
    return {
        "x": nrm(ks[0], (BATCH, SEQ, D_MODEL), 1.0),
        "attn_norm_g": gain(ks[1], (N_ATTN_LAYERS, D_MODEL)),
        "w_qkv": nrm(ks[2], (N_ATTN_LAYERS, D_MODEL, 2 * QK_WIDTH + V_WIDTH), D_MODEL ** -0.5),
        "w_o_attn": nrm(ks[3], (N_ATTN_LAYERS, V_WIDTH, D_MODEL), V_WIDTH ** -0.5 * res_scale),
        "lambda_q1": nrm(ks[4], (N_ATTN_LAYERS, HEAD_DIM), 0.1),
        "lambda_k1": nrm(ks[5], (N_ATTN_LAYERS, HEAD_DIM), 0.1),
        "lambda_q2": nrm(ks[6], (N_ATTN_LAYERS, HEAD_DIM), 0.1),
        "lambda_k2": nrm(ks[7], (N_ATTN_LAYERS, HEAD_DIM), 0.1),
        "subln_g": gain(ks[8], (N_ATTN_LAYERS, V_HEAD_DIM)),
        "conv_norm_g": gain(ks[9], (N_CONV_LAYERS, D_MODEL)),
        "w_bch": nrm(ks[10], (N_CONV_LAYERS, D_MODEL, 3 * D_MODEL), D_MODEL ** -0.5),
        "conv_w": nrm(ks[11], (N_CONV_LAYERS, CONV_WIDTH, D_MODEL), CONV_WIDTH ** -0.5),
        "w_o_conv": nrm(ks[12], (N_CONV_LAYERS, D_MODEL, D_MODEL), D_MODEL ** -0.5 * res_scale),
        "ffn_norm_g": gain(ks[13], (DEPTH, D_MODEL)),
        "w_gate": nrm(ks[14], (DEPTH, D_MODEL, D_FF), D_MODEL ** -0.5),
        "w_up": nrm(ks[15], (DEPTH, D_MODEL, D_FF), D_MODEL ** -0.5),
        "w_down": nrm(ks[16], (DEPTH, D_FF, D_MODEL), D_FF ** -0.5 * res_scale),
        "final_norm_g": gain(ks[17], (D_MODEL,)),
    }


def reference(x, attn_norm_g, w_qkv, w_o_attn, lambda_q1, lambda_k1, lambda_q2, lambda_k2,
              subln_g, conv_norm_g, w_bch, conv_w, w_o_conv, ffn_norm_g, w_gate, w_up,
              w_down, final_norm_g):
    cos, sin = rope_tables(x.shape[1])
    h = x
    for i in range(DEPTH):
        j = i // N_MIXERS
        if i % N_MIXERS == 0:
            lambda_init = 0.8 - 0.6 * math.exp(-0.3 * i)
            h = h + diff_attention(rms_norm(h, attn_norm_g[j], RMS_EPS), w_qkv[j], w_o_attn[j],
                                   lambda_q1[j], lambda_k1[j], lambda_q2[j], lambda_k2[j],
                                   subln_g[j], lambda_init, cos, sin)
        else:
            h = h + short_conv_mixer(rms_norm(h, conv_norm_g[j], RMS_EPS), w_bch[j], conv_w[j],
                                     w_o_conv[j])
        h = h + swiglu(rms_norm(h, ffn_norm_g[i], RMS_EPS), w_gate[i], w_up[i], w_down[i])
    return rms_norm(h, final_norm_g, RMS_EPS)
```

```python
import functools
import math

import jax
import jax.numpy as jnp
from jax import lax
from jax.experimental import pallas as pl
from jax.experimental.pallas import tpu as pltpu

D_MODEL = 2048
DEPTH = 4
N_DIFF_HEADS = 8
HEAD_DIM = 128
V_HEAD_DIM = 2 * HEAD_DIM
QK_WIDTH = 2 * N_DIFF_HEADS * HEAD_DIM
ROPE_THETA = 10000.0
CONV_WIDTH = 3
RMS_EPS = 1e-6
SUBLN_EPS = 1e-5

V7X_LANES = 128
V7X_F32_SUBLANES = 8
V7X_VMEM_BYTES = 64 * 1024 * 1024
VMEM_LIMIT_BYTES = V7X_VMEM_BYTES - 6 * 1024 * 1024

MASK_VALUE = -1e30

F32 = jnp.float32
BF16 = jnp.bfloat16


def _params(semantics):
    return pltpu.CompilerParams(dimension_semantics=semantics,
                                vmem_limit_bytes=VMEM_LIMIT_BYTES)


def _rms_norm_rows(x, g, eps):
    ms = jnp.mean(x * x, axis=-1, keepdims=True)
    return x * lax.rsqrt(ms + eps) * g


def _qkv_kernel(x_ref, g_ref, w_ref, ra_ref, rb_ref, o_ref, xn_ref, *, tn):
    j = pl.program_id(1)
    n_q_tiles = QK_WIDTH // tn
    n_rope_tiles = 2 * n_q_tiles

    @pl.when(j == 0)
    def _():
        xn_ref[...] = _rms_norm_rows(x_ref[...], g_ref[...], RMS_EPS).astype(BF16)

    acc = jnp.dot(xn_ref[...], w_ref[...], preferred_element_type=F32)

    @pl.when(j < n_rope_tiles)
    def _():
        scale = jnp.where(j < n_q_tiles, HEAD_DIM ** -0.5, 1.0).astype(F32)
        ra = ra_ref[...] * scale
        rb = rb_ref[...] * scale
        for c in range(tn // HEAD_DIM):
            t = acc[:, c * HEAD_DIM:(c + 1) * HEAD_DIM]
            r = t * ra + pltpu.roll(t, HEAD_DIM // 2, axis=1) * rb
            o_ref[:, c * HEAD_DIM:(c + 1) * HEAD_DIM] = r.astype(BF16)

    @pl.when(j >= n_rope_tiles)
    def _():
        o_ref[...] = acc.astype(BF16)


def _qkv_proj(h, g, w, ra, rb, *, seq, tm=1024, tn=1024):
    t, d = h.shape
    n = w.shape[1]
    seq_tiles = seq // tm
    return pl.pallas_call(
        functools.partial(_qkv_kernel, tn=tn),
        grid=(t // tm, n // tn),
        in_specs=[
            pl.BlockSpec((tm, d), lambda i, j: (i, 0)),
            pl.BlockSpec((1, d), lambda i, j: (0, 0)),
            pl.BlockSpec((d, tn), lambda i, j: (0, j)),
            pl.BlockSpec((tm, HEAD_DIM), lambda i, j: (i % seq_tiles, 0)),
            pl.BlockSpec((tm, HEAD_DIM), lambda i, j: (i % seq_tiles, 0)),
        ],
        out_specs=pl.BlockSpec((tm, tn), lambda i, j: (i, j)),
        out_shape=jax.ShapeDtypeStruct((t, n), BF16),
        scratch_shapes=[pltpu.VMEM((tm, d), BF16)],
        compiler_params=_params(("parallel", "arbitrary")),
        name="qkv_proj",
    )(h, g, w, ra, rb)


def _attn_kernel(q_ref, k_ref, v_ref, lq1_ref, lk1_ref, lq2_ref, lk2_ref, sg_ref,
                 o_ref, m_ref, l_ref, acc_ref, *, tq, tk, lambda_init):
    qi = pl.program_id(2)

    m_ref[...] = jnp.full(m_ref.shape, MASK_VALUE, F32)
    l_ref[...] = jnp.zeros(l_ref.shape, F32)
    acc_ref[...] = jnp.zeros(acc_ref.shape, F32)

    def chunk(kv_start, masked):
        k = k_ref[pl.ds(kv_start, tk), :]
        v = v_ref[pl.ds(kv_start, tk), :]
        for c in range(2):
            q = q_ref[:, c * HEAD_DIM:(c + 1) * HEAD_DIM]
            s = lax.dot_general(q, k[:, c * HEAD_DIM:(c + 1) * HEAD_DIM],
                                (((1,), (1,)), ((), ())),
                                preferred_element_type=F32)
            if masked:
                row = lax.broadcasted_iota(jnp.int32, (tq, tk), 0)
                col = lax.broadcasted_iota(jnp.int32, (tq, tk), 1)
                s = jnp.where(col <= row, s, MASK_VALUE)
            m_prev = m_ref[c]
            m_new = jnp.maximum(m_prev, jnp.max(s, axis=-1, keepdims=True))
            alpha = jnp.exp(m_prev - m_new)
            p = jnp.exp(s - m_new[:, :1])
            l_ref[c] = alpha * l_ref[c] + jnp.sum(p, axis=-1, keepdims=True)
            m_ref[c] = m_new
            pv = jnp.dot(p.astype(BF16), v, preferred_element_type=F32)
            acc_ref[c] = alpha[:, :1] * acc_ref[c] + pv

    def body(kj, carry):
        chunk(pl.multiple_of(kj * tk, tk), masked=False)
        return carry

    lax.fori_loop(0, qi, body, 0)
    chunk(pl.multiple_of(qi * tk, tk), masked=True)

    lam = (jnp.exp(jnp.sum(lq1_ref[...] * lk1_ref[...], axis=-1, keepdims=True))
           - jnp.exp(jnp.sum(lq2_ref[...] * lk2_ref[...], axis=-1, keepdims=True))
           + lambda_init)
    o1 = acc_ref[0] / l_ref[0][:, :1]
    o2 = acc_ref[1] / l_ref[1][:, :1]
    o = o1 - lam * o2
    o = _rms_norm_rows(o, sg_ref[...], SUBLN_EPS) * (1.0 - lambda_init)
    o_ref[...] = o.astype(BF16)


def _diff_attention(qkv, lq1, lk1, lq2, lk2, subln_g, *, batch, seq, lambda_init,
                    tq=512):
    t = qkv.shape[0]
    tk = tq
    q_tiles = seq // tq
    vec = lambda a: a.reshape(1, -1)
    small = lambda width: pl.BlockSpec((1, width), lambda b, h, i: (0, 0))
    return pl.pallas_call(
        functools.partial(_attn_kernel, tq=tq, tk=tk, lambda_init=lambda_init),
        grid=(batch, N_DIFF_HEADS, q_tiles),
        in_specs=[
            pl.BlockSpec((tq, V_HEAD_DIM), lambda b, h, i: (b * q_tiles + i, h)),
            pl.BlockSpec((seq, V_HEAD_DIM), lambda b, h, i: (b, N_DIFF_HEADS + h)),
            pl.BlockSpec((seq, V_HEAD_DIM), lambda b, h, i: (b, 2 * N_DIFF_HEADS + h)),
            small(HEAD_DIM), small(HEAD_DIM), small(HEAD_DIM), small(HEAD_DIM),
            small(V_HEAD_DIM),
        ],
        out_specs=pl.BlockSpec((tq, V_HEAD_DIM), lambda b, h, i: (b * q_tiles + i, h)),
        out_shape=jax.ShapeDtypeStruct((t, N_DIFF_HEADS * V_HEAD_DIM), BF16),
        scratch_shapes=[
            pltpu.VMEM((2, tq, V7X_LANES), F32),
            pltpu.VMEM((2, tq, V7X_LANES), F32),
            pltpu.VMEM((2, tq, V_HEAD_DIM), F32),
        ],
        compiler_params=_params(("parallel", "parallel", "arbitrary")),
        name="diff_attention",
    )(qkv, qkv, qkv, vec(lq1), vec(lk1), vec(lq2), vec(lk2), vec(subln_g))


def _proj_residual_kernel(a_ref, w_ref, r_ref, o_ref):
    o_ref[...] = r_ref[...] + jnp.dot(a_ref[...], w_ref[...], preferred_element_type=F32)


def _proj_residual(a, w, res, *, tm=512, tn=1024):
    t, k = a.shape
    n = w.shape[1]
    return pl.pallas_call(
        _proj_residual_kernel,
        grid=(t // tm, n // tn),
        in_specs=[
            pl.BlockSpec((tm, k), lambda i, j: (i, 0)),
            pl.BlockSpec((k, tn), lambda i, j: (0, j)),
            pl.BlockSpec((tm, tn), lambda i, j: (i, j)),
        ],
        out_specs=pl.BlockSpec((tm, tn), lambda i, j: (i, j)),
        out_shape=jax.ShapeDtypeStruct((t, n), F32),
        compiler_params=_params(("parallel", "parallel")),
        name="proj_residual",
    )(a, w, res)


def _bch_kernel(x_ref, g_ref, wb_ref, wc_ref, wu_ref, gb_ref, z_ref, xn_ref):
    @pl.when(pl.program_id(1) == 0)
    def _():
        xn_ref[...] = _rms_norm_rows(x_ref[...], g_ref[...], RMS_EPS).astype(BF16)

    xn = xn_ref[...]
    gb_ref[...] = jnp.dot(xn, wb_ref[...], preferred_element_type=F32).astype(BF16)
    gate_c = jnp.dot(xn, wc_ref[...], preferred_element_type=F32)
    u = jnp.dot(xn, wu_ref[...], preferred_element_type=F32)
    z_ref[...] = (gate_c * u).astype(BF16)


def _bch_proj(h, g, w, *, tm=1024, tn=512):
    t, d = h.shape
    col_tiles = d // tn
    w_spec = lambda part: pl.BlockSpec((d, tn), lambda i, j: (0, part * col_tiles + j))
    out_spec = pl.BlockSpec((tm, tn), lambda i, j: (i, j))
    return pl.pallas_call(
        _bch_kernel,
        grid=(t // tm, col_tiles),
        in_specs=[
            pl.BlockSpec((tm, d), lambda i, j: (i, 0)),
            pl.BlockSpec((1, d), lambda i, j: (0, 0)),
            w_spec(0), w_spec(1), w_spec(2),
        ],
        out_specs=[out_spec, out_spec],
        out_shape=[jax.ShapeDtypeStruct((t, d), BF16)] * 2,
        scratch_shapes=[pltpu.VMEM((tm, d), BF16)],
        compiler_params=_params(("parallel", "arbitrary")),
        name="bch_proj",
    )(h, g, w, w, w)


def _conv_out_kernel(gb_ref, z_ref, halo_ref, cw_ref, w_ref, r_ref, o_ref, a_ref, *,
                     seq_tiles):
    i = pl.program_id(0)

    @pl.when(pl.program_id(1) == 0)
    def _():
        tm = z_ref.shape[0]
        z = z_ref[...].astype(F32)
        halo = jnp.where(i % seq_tiles == 0, 0.0, halo_ref[...].astype(F32))
        zext = jnp.concatenate([halo, z], axis=0)
        pad = halo.shape[0]
        cw = cw_ref[...]
        zc = cw[2:3, :] * z
        zc = zc + cw[1:2, :] * zext[pad - 1:pad - 1 + tm, :]
        zc = zc + cw[0:1, :] * zext[pad - 2:pad - 2 + tm, :]
        a_ref[...] = (gb_ref[...].astype(F32) * zc).astype(BF16)

    o_ref[...] = r_ref[...] + jnp.dot(a_ref[...], w_ref[...], preferred_element_type=F32)


def _conv_out_proj(gate_b, z, conv_w, w, res, *, seq, tm=512, tn=1024):
    t, d = z.shape
    n = w.shape[1]
    halo_rows = 2 * V7X_F32_SUBLANES
    halo_per_tile = tm // halo_rows
    row_spec = pl.BlockSpec((tm, d), lambda i, j: (i, 0))
    return pl.pallas_call(
        functools.partial(_conv_out_kernel, seq_tiles=seq // tm),
        grid=(t // tm, n // tn),
        in_specs=[
            row_spec,
            row_spec,
            pl.BlockSpec((halo_rows, d),
                         lambda i, j: (jnp.maximum(i * halo_per_tile - 1, 0), 0)),
            pl.BlockSpec((CONV_WIDTH, d), lambda i, j: (0, 0)),
            pl.BlockSpec((d, tn), lambda i, j: (0, j)),
            pl.BlockSpec((tm, tn), lambda i, j: (i, j)),
        ],
        out_specs=pl.BlockSpec((tm, tn), lambda i, j: (i, j)),
        out_shape=jax.ShapeDtypeStruct((t, n), F32),
        scratch_shapes=[pltpu.VMEM((tm, d), BF16)],
        compiler_params=_params(("parallel", "arbitrary")),
        name="conv_out_proj",
    )(gate_b, z, z, conv_w, w, res)


def _ffn_in_kernel(x_ref, g_ref, wg_ref, wu_ref, o_ref, xn_ref):
    @pl.when(pl.program_id(1) == 0)
    def _():
        xn_ref[...] = _rms_norm_rows(x_ref[...], g_ref[...], RMS_EPS).astype(BF16)

    xn = xn_ref[...]
    gate = jnp.dot(xn, wg_ref[...], preferred_element_type=F32)
    up = jnp.dot(xn, wu_ref[...], preferred_element_type=F32)
    o_ref[...] = (gate * jax.nn.sigmoid(gate) * up).astype(BF16)


def _ffn_in(h, g, w_gate, w_up, *, tm=1024, tf=512):
    t, d = h.shape
    f = w_gate.shape[1]
    return pl.pallas_call(
        _ffn_in_kernel,
        grid=(t // tm, f // tf),
        in_specs=[
            pl.BlockSpec((tm, d), lambda i, j: (i, 0)),
            pl.BlockSpec((1, d), lambda i, j: (0, 0)),
            pl.BlockSpec((d, tf), lambda i, j: (0, j)),
            pl.BlockSpec((d, tf), lambda i, j: (0, j)),
        ],
        out_specs=pl.BlockSpec((tm, tf), lambda i, j: (i, j)),
        out_shape=jax.ShapeDtypeStruct((t, f), BF16),
        scratch_shapes=[pltpu.VMEM((tm, d), BF16)],
        compiler_params=_params(("parallel", "arbitrary")),
        name="ffn_in",
    )(h, g, w_gate, w_up)


def _final_norm_kernel(x_ref, g_ref, o_ref):
    o_ref[...] = _rms_norm_rows(x_ref[...], g_ref[...], RMS_EPS)


def _final_norm(h, g, *, tm=1024):
    t, d = h.shape
    return pl.pallas_call(
        _final_norm_kernel,
        grid=(t // tm,),
        in_specs=[pl.BlockSpec((tm, d), lambda i: (i, 0)),
                  pl.BlockSpec((1, d), lambda i: (0, 0))],
        out_specs=pl.BlockSpec((tm, d), lambda i: (i, 0)),
        out_shape=jax.ShapeDtypeStruct((t, d), F32),
        compiler_params=_params(("parallel",)),
        name="final_norm",
    )(h, g)


def _rope_tables(seq):
    pos = jnp.arange(seq, dtype=F32)
    inv_freq = 1.0 / (ROPE_THETA ** (jnp.arange(0, HEAD_DIM, 2, dtype=F32) / HEAD_DIM))
    ang = pos[:, None] * inv_freq[None, :]
    cos, sin = jnp.cos(ang), jnp.sin(ang)
    return (jnp.concatenate([cos, cos], axis=-1),
            jnp.concatenate([-sin, sin], axis=-1))


def kernel(x, attn_norm_g, w_qkv, w_o_attn, lambda_q1, lambda_k1, lambda_q2, lambda_k2,
           subln_g, conv_norm_g, w_bch, conv_w, w_o_conv, ffn_norm_g, w_gate, w_up,
           w_down, final_norm_g):
    batch, seq, d = x.shape
    rope_a, rope_b = _rope_tables(seq)
    row = lambda a: a.reshape(1, -1)
    h = x.reshape(batch * seq, d)
    for i in range(DEPTH):
        j = i // 2
        if i % 2 == 0:
            lambda_init = 0.8 - 0.6 * math.exp(-0.3 * i)
            qkv = _qkv_proj(h, row(attn_norm_g[j]), w_qkv[j].astype(BF16), rope_a, rope_b,
                            seq=seq)
            o = _diff_attention(qkv, lambda_q1[j], lambda_k1[j], lambda_q2[j], lambda_k2[j],
                                subln_g[j], batch=batch, seq=seq, lambda_init=lambda_init)
            h = _proj_residual(o, w_o_attn[j].astype(BF16), h)
        else:
            gate_b, z = _bch_proj(h, row(conv_norm_g[j]), w_bch[j].astype(BF16))
            h = _conv_out_proj(gate_b, z, conv_w[j], w_o_conv[j].astype(BF16), h, seq=seq)
        hid = _ffn_in(h, row(ffn_norm_g[i]), w_gate[i].astype(BF16), w_up[i].astype(BF16))
        h = _proj_residual(hid, w_down[i].astype(BF16), h)
    return _final_norm(h, row(final_norm_g)).reshape(batch, seq, d)
```

```python
import functools
import math

import jax
import jax.numpy as jnp
from jax import lax
from jax.experimental import pallas as pl
from jax.experimental.pallas import tpu as pltpu

D_MODEL = 2048
DEPTH = 4
N_DIFF_HEADS = 8
HEAD_DIM = 128
V_HEAD_DIM = 2 * HEAD_DIM
QK_WIDTH = 2 * N_DIFF_HEADS * HEAD_DIM
ROPE_THETA = 10000.0
CONV_WIDTH = 3
RMS_EPS = 1e-6
SUBLN_EPS = 1e-5

V7X_LANES = 128
V7X_F32_SUBLANES = 8
V7X_VMEM_BYTES = 64 * 1024 * 1024
VMEM_LIMIT_BYTES = V7X_VMEM_BYTES - 6 * 1024 * 1024

MASK_VALUE = -1e30
LOG2_E = math.log2(math.e)

F32 = jnp.float32
BF16 = jnp.bfloat16


def _params(semantics):
    return pltpu.CompilerParams(dimension_semantics=semantics,
                                vmem_limit_bytes=VMEM_LIMIT_BYTES)


def _run_if(cond, fn):
    def body(_, carry):
        fn()
        return carry
    lax.fori_loop(0, cond.astype(jnp.int32), body, 0)


def _rms_norm_rows(x, g, eps):
    ms = jnp.mean(x * x, axis=-1, keepdims=True)
    return x * lax.rsqrt(ms + eps) * g


def _qv_kernel(x_ref, g_ref, w_ref, ra_ref, rb_ref, o_ref, xn_ref, *, tn):
    j = pl.program_id(1)
    n_q_tiles = QK_WIDTH // tn

    @pl.when(j == 0)
    def _():
        xn_ref[...] = _rms_norm_rows(x_ref[...], g_ref[...], RMS_EPS).astype(BF16)

    acc = jnp.dot(xn_ref[...], w_ref[...], preferred_element_type=F32)

    @pl.when(j < n_q_tiles)
    def _():
        scale = LOG2_E * HEAD_DIM ** -0.5
        ra = ra_ref[...] * scale
        rb = rb_ref[...] * scale
        for c in range(tn // HEAD_DIM):
            t = acc[:, c * HEAD_DIM:(c + 1) * HEAD_DIM]
            r = t * ra + pltpu.roll(t, HEAD_DIM // 2, axis=1) * rb
            o_ref[:, c * HEAD_DIM:(c + 1) * HEAD_DIM] = r.astype(BF16)

    @pl.when(j >= n_q_tiles)
    def _():
        o_ref[...] = acc.astype(BF16)


def _qv_proj(h, g, w, ra, rb, *, seq, tm=1024, tn=1024):
    t, d = h.shape
    seq_tiles = seq // tm
    n_q_tiles = QK_WIDTH // tn
    w_col = lambda j: jnp.where(j < n_q_tiles, j, j + n_q_tiles)
    return pl.pallas_call(
        functools.partial(_qv_kernel, tn=tn),
        grid=(t // tm, 2 * n_q_tiles),
        in_specs=[
            pl.BlockSpec((tm, d), lambda i, j: (i, 0)),
            pl.BlockSpec((1, d), lambda i, j: (0, 0)),
            pl.BlockSpec((d, tn), lambda i, j: (0, w_col(j))),
            pl.BlockSpec((tm, HEAD_DIM), lambda i, j: (i % seq_tiles, 0)),
            pl.BlockSpec((tm, HEAD_DIM), lambda i, j: (i % seq_tiles, 0)),
        ],
        out_specs=pl.BlockSpec((tm, tn), lambda i, j: (i, j)),
        out_shape=jax.ShapeDtypeStruct((t, 2 * QK_WIDTH), BF16),
        scratch_shapes=[pltpu.VMEM((tm, d), BF16)],
        compiler_params=_params(("parallel", "arbitrary")),
        name="qv_proj",
    )(h, g, w, ra, rb)


def _kt_kernel(x_ref, g_ref, wt_ref, rat_ref, rbt_ref, o_ref, xn_ref, *, tn):
    @pl.when(pl.program_id(1) == 0)
    def _():
        xn_ref[...] = _rms_norm_rows(x_ref[...], g_ref[...], RMS_EPS).astype(BF16)

    acc = lax.dot_general(wt_ref[...], xn_ref[...], (((1,), (1,)), ((), ())),
                          preferred_element_type=F32)
    rat = rat_ref[...]
    rbt = rbt_ref[...]
    half = HEAD_DIM // 2
    for c in range(tn // HEAD_DIM):
        t = acc[c * HEAD_DIM:(c + 1) * HEAD_DIM, :]
        swapped = jnp.concatenate([t[half:, :], t[:half, :]], axis=0)
        o_ref[c * HEAD_DIM:(c + 1) * HEAD_DIM, :] = (t * rat + swapped * rbt).astype(BF16)


def _kt_proj(h, g, wt, rat, rbt, *, seq, tm=1024, tn=1024):
    t, d = h.shape
    n = wt.shape[0]
    seq_tiles = seq // tm
    return pl.pallas_call(
        functools.partial(_kt_kernel, tn=tn),
        grid=(t // tm, n // tn),
        in_specs=[
            pl.BlockSpec((tm, d), lambda i, j: (i, 0)),
            pl.BlockSpec((1, d), lambda i, j: (0, 0)),
            pl.BlockSpec((tn, d), lambda i, j: (j, 0)),
            pl.BlockSpec((HEAD_DIM, tm), lambda i, j: (0, i % seq_tiles)),
            pl.BlockSpec((HEAD_DIM, tm), lambda i, j: (0, i % seq_tiles)),
        ],
        out_specs=pl.BlockSpec((tn, tm), lambda i, j: (j, i)),
        out_shape=jax.ShapeDtypeStruct((n, t), BF16),
        scratch_shapes=[pltpu.VMEM((tm, d), BF16)],
        compiler_params=_params(("parallel", "arbitrary")),
        name="kt_proj",
    )(h, g, wt, rat, rbt)


def _attn_kernel(q_ref, kt_ref, v_ref, lq1_ref, lk1_ref, lq2_ref, lk2_ref, sg_ref,
                 o_ref, s0_ref, s1_ref, p0_ref, p1_ref, a0_ref, a1_ref,
                 m_ref, l_ref, acc_ref, *, tq, tk, lambda_init):
    qi = pl.program_id(2)
    lane_blocks = tk // V7X_LANES
    s_refs, p_refs, alpha_refs = (s0_ref, s1_ref), (p0_ref, p1_ref), (a0_ref, a1_ref)

    m_ref[...] = jnp.full(m_ref.shape, MASK_VALUE, F32)
    l_ref[...] = jnp.zeros(l_ref.shape, F32)
    acc_ref[...] = jnp.zeros(acc_ref.shape, F32)

    def kv_start(kj):
        return pl.multiple_of(kj * tk, tk)

    def scores(kj, slot):
        for c in range(2):
            rows = slice(c * HEAD_DIM, (c + 1) * HEAD_DIM)
            s_refs[slot][c] = jnp.dot(q_ref[:, rows],
                                      kt_ref[rows, pl.ds(kv_start(kj), tk)],
                                      preferred_element_type=F32)

    def softmax(slot, masked):
        for c in range(2):
            s = s_refs[slot][c]
            if masked:
                row = lax.broadcasted_iota(jnp.int32, (tq, tk), 0)
                col = lax.broadcasted_iota(jnp.int32, (tq, tk), 1)
                s = jnp.where(col <= row, s, MASK_VALUE)
            blocks = [s[:, b * V7X_LANES:(b + 1) * V7X_LANES] for b in range(lane_blocks)]
            m_prev = m_ref[c]
            lane_max = functools.reduce(jnp.maximum, blocks)
            m_new = jnp.maximum(m_prev, jnp.max(lane_max, axis=-1, keepdims=True))
            alpha = jnp.exp2(m_prev - m_new)
            ps = [jnp.exp2(blk - m_new) for blk in blocks]
            l_ref[c] = alpha * l_ref[c] + functools.reduce(jnp.add, ps)
            m_ref[c] = m_new
            alpha_refs[slot][c] = alpha
            p_refs[slot][c] = jnp.concatenate(ps, axis=1).astype(BF16)

    def accumulate(kj, slot):
        v = v_ref[pl.ds(kv_start(kj), tk), :]
        for c in range(2):
            alpha = alpha_refs[slot][c]
            pv = jnp.dot(p_refs[slot][c], v, preferred_element_type=F32)
            acc_ref[c] = jnp.concatenate([alpha, alpha], axis=1) * acc_ref[c] + pv

    def pipeline_step(kj, slot):
        scores(kj + 2, slot)
        softmax(1 - slot, masked=False)
        accumulate(kj, slot)

    scores(0, 0)

    def fill():
        scores(1, 1)
        softmax(0, masked=False)

    _run_if(qi >= 1, fill)

    steady_steps = qi - 1

    def pair(u, carry):
        pipeline_step(2 * u, 0)
        pipeline_step(2 * u + 1, 1)
        return carry

    lax.fori_loop(0, steady_steps // 2, pair, 0)

    qi_even = qi % 2 == 0
    _run_if(jnp.logical_and(qi >= 2, qi_even), lambda: pipeline_step(qi - 2, 0))

    def drain(slot):
        softmax(slot, masked=True)
        _run_if(qi >= 1, lambda: accumulate(qi - 1, 1 - slot))
        accumulate(qi, slot)

    _run_if(qi_even, lambda: drain(0))
    _run_if(jnp.logical_not(qi_even), lambda: drain(1))

    lam = (jnp.exp(jnp.sum(lq1_ref[...] * lk1_ref[...], axis=-1, keepdims=True))
           - jnp.exp(jnp.sum(lq2_ref[...] * lk2_ref[...], axis=-1, keepdims=True))
           + lambda_init)
    o1 = acc_ref[0] / jnp.sum(l_ref[0], axis=-1, keepdims=True)
    o2 = acc_ref[1] / jnp.sum(l_ref[1], axis=-1, keepdims=True)
    o = o1 - lam * o2
    o = _rms_norm_rows(o, sg_ref[...], SUBLN_EPS) * (1.0 - lambda_init)
    o_ref[...] = o.astype(BF16)


def _diff_attention(qv, kt, lq1, lk1, lq2, lk2, subln_g, *, batch, seq, lambda_init,
                    tq=512):
    t = qv.shape[0]
    tk = tq
    q_tiles = seq // tq
    vec = lambda a: a.reshape(1, -1)
    small = lambda width: pl.BlockSpec((1, width), lambda b, h, i: (0, 0))
    return pl.pallas_call(
        functools.partial(_attn_kernel, tq=tq, tk=tk, lambda_init=lambda_init),
        grid=(batch, N_DIFF_HEADS, q_tiles),
        in_specs=[
            pl.BlockSpec((tq, V_HEAD_DIM), lambda b, h, i: (b * q_tiles + i, h)),
            pl.BlockSpec((V_HEAD_DIM, seq), lambda b, h, i: (h, b)),
            pl.BlockSpec((seq, V_HEAD_DIM), lambda b, h, i: (b, N_DIFF_HEADS + h)),
            small(HEAD_DIM), small(HEAD_DIM), small(HEAD_DIM), small(HEAD_DIM),
            small(V_HEAD_DIM),
        ],
        out_specs=pl.BlockSpec((tq, V_HEAD_DIM), lambda b, h, i: (b * q_tiles + i, h)),
        out_shape=jax.ShapeDtypeStruct((t, N_DIFF_HEADS * V_HEAD_DIM), BF16),
        scratch_shapes=[
            pltpu.VMEM((2, tq, tk), F32), pltpu.VMEM((2, tq, tk), F32),
            pltpu.VMEM((2, tq, tk), BF16), pltpu.VMEM((2, tq, tk), BF16),
            pltpu.VMEM((2, tq, V7X_LANES), F32), pltpu.VMEM((2, tq, V7X_LANES), F32),
            pltpu.VMEM((2, tq, V7X_LANES), F32),
            pltpu.VMEM((2, tq, V7X_LANES), F32),
            pltpu.VMEM((2, tq, V_HEAD_DIM), F32),
        ],
        compiler_params=_params(("parallel", "parallel", "arbitrary")),
        name="diff_attention",
    )(qv, kt, qv, vec(lq1), vec(lk1), vec(lq2), vec(lk2), vec(subln_g))


def _proj_residual_kernel(a_ref, w_ref, r_ref, o_ref):
    o_ref[...] = r_ref[...] + jnp.dot(a_ref[...], w_ref[...], preferred_element_type=F32)


def _proj_residual(a, w, res, *, tm=512, tn=1024):
    t, k = a.shape
    n = w.shape[1]
    return pl.pallas_call(
        _proj_residual_kernel,
        grid=(t // tm, n // tn),
        in_specs=[
            pl.BlockSpec((tm, k), lambda i, j: (i, 0)),
            pl.BlockSpec((k, tn), lambda i, j: (0, j)),
            pl.BlockSpec((tm, tn), lambda i, j: (i, j)),
        ],
        out_specs=pl.BlockSpec((tm, tn), lambda i, j: (i, j)),
        out_shape=jax.ShapeDtypeStruct((t, n), F32),
        compiler_params=_params(("parallel", "parallel")),
        name="proj_residual",
    )(a, w, res)


def _bch_kernel(x_ref, g_ref, wb_ref, wc_ref, wu_ref, gb_ref, z_ref, xn_ref):
    @pl.when(pl.program_id(1) == 0)
    def _():
        xn_ref[...] = _rms_norm_rows(x_ref[...], g_ref[...], RMS_EPS).astype(BF16)

    xn = xn_ref[...]
    gb_ref[...] = jnp.dot(xn, wb_ref[...], preferred_element_type=F32).astype(BF16)
    gate_c = jnp.dot(xn, wc_ref[...], preferred_element_type=F32)
    u = jnp.dot(xn, wu_ref[...], preferred_element_type=F32)
    z_ref[...] = (gate_c * u).astype(BF16)


def _bch_proj(h, g, w, *, tm=1024, tn=512):
    t, d = h.shape
    col_tiles = d // tn
    w_spec = lambda part: pl.BlockSpec((d, tn), lambda i, j: (0, part * col_tiles + j))
    out_spec = pl.BlockSpec((tm, tn), lambda i, j: (i, j))
    return pl.pallas_call(
        _bch_kernel,
        grid=(t // tm, col_tiles),
        in_specs=[
            pl.BlockSpec((tm, d), lambda i, j: (i, 0)),
            pl.BlockSpec((1, d), lambda i, j: (0, 0)),
            w_spec(0), w_spec(1), w_spec(2),
        ],
        out_specs=[out_spec, out_spec],
        out_shape=[jax.ShapeDtypeStruct((t, d), BF16)] * 2,
        scratch_shapes=[pltpu.VMEM((tm, d), BF16)],
        compiler_params=_params(("parallel", "arbitrary")),
        name="bch_proj",
    )(h, g, w, w, w)


def _conv_out_kernel(gb_ref, z_ref, halo_ref, cw_ref, w_ref, r_ref, o_ref, a_ref, *,
                     seq_tiles):
    i = pl.program_id(0)

    @pl.when(pl.program_id(1) == 0)
    def _():
        tm = z_ref.shape[0]
        z = z_ref[...].astype(F32)
        halo = jnp.where(i % seq_tiles == 0, 0.0, halo_ref[...].astype(F32))
        zext = jnp.concatenate([halo, z], axis=0)
        pad = halo.shape[0]
        cw = cw_ref[...]
        zc = cw[2:3, :] * z
        zc = zc + cw[1:2, :] * zext[pad - 1:pad - 1 + tm, :]
        zc = zc + cw[0:1, :] * zext[pad - 2:pad - 2 + tm, :]
        a_ref[...] = (gb_ref[...].astype(F32) * zc).astype(BF16)

    o_ref[...] = r_ref[...] + jnp.dot(a_ref[...], w_ref[...], preferred_element_type=F32)


def _conv_out_proj(gate_b, z, conv_w, w, res, *, seq, tm=512, tn=1024):
    t, d = z.shape
    n = w.shape[1]
    halo_rows = 2 * V7X_F32_SUBLANES
    halo_per_tile = tm // halo_rows
    row_spec = pl.BlockSpec((tm, d), lambda i, j: (i, 0))
    return pl.pallas_call(
        functools.partial(_conv_out_kernel, seq_tiles=seq // tm),
        grid=(t // tm, n // tn),
        in_specs=[
            row_spec,
            row_spec,
            pl.BlockSpec((halo_rows, d),
                         lambda i, j: (jnp.maximum(i * halo_per_tile - 1, 0), 0)),
            pl.BlockSpec((CONV_WIDTH, d), lambda i, j: (0, 0)),
            pl.BlockSpec((d, tn), lambda i, j: (0, j)),
            pl.BlockSpec((tm, tn), lambda i, j: (i, j)),
        ],
        out_specs=pl.BlockSpec((tm, tn), lambda i, j: (i, j)),
        out_shape=jax.ShapeDtypeStruct((t, n), F32),
        scratch_shapes=[pltpu.VMEM((tm, d), BF16)],
        compiler_params=_params(("parallel", "arbitrary")),
        name="conv_out_proj",
    )(gate_b, z, z, conv_w, w, res)


def _ffn_in_kernel(x_ref, g_ref, wg_ref, wu_ref, o_ref, xn_ref):
    @pl.when(pl.program_id(1) == 0)
    def _():
        xn_ref[...] = _rms_norm_rows(x_ref[...], g_ref[...], RMS_EPS).astype(BF16)

    xn = xn_ref[...]
    gate = jnp.dot(xn, wg_ref[...], preferred_element_type=F32)
    up = jnp.dot(xn, wu_ref[...], preferred_element_type=F32)
    o_ref[...] = (gate * jax.nn.sigmoid(gate) * up).astype(BF16)


def _ffn_in(h, g, w_gate, w_up, *, tm=1024, tf=512):
    t, d = h.shape
    f = w_gate.shape[1]
    return pl.pallas_call(
        _ffn_in_kernel,
        grid=(t // tm, f // tf),
        in_specs=[
            pl.BlockSpec((tm, d), lambda i, j: (i, 0)),
            pl.BlockSpec((1, d), lambda i, j: (0, 0)),
            pl.BlockSpec((d, tf), lambda i, j: (0, j)),
            pl.BlockSpec((d, tf), lambda i, j: (0, j)),
        ],
        out_specs=pl.BlockSpec((tm, tf), lambda i, j: (i, j)),
        out_shape=jax.ShapeDtypeStruct((t, f), BF16),
        scratch_shapes=[pltpu.VMEM((tm, d), BF16)],
        compiler_params=_params(("parallel", "arbitrary")),
        name="ffn_in",
    )(h, g, w_gate, w_up)


def _final_norm_kernel(x_ref, g_ref, o_ref):
    o_ref[...] = _rms_norm_rows(x_ref[...], g_ref[...], RMS_EPS)


def _final_norm(h, g, *, tm=1024):
    t, d = h.shape
    return pl.pallas_call(
        _final_norm_kernel,
        grid=(t // tm,),
        in_specs=[pl.BlockSpec((tm, d), lambda i: (i, 0)),
                  pl.BlockSpec((1, d), lambda i: (0, 0))],
        out_specs=pl.BlockSpec((tm, d), lambda i: (i, 0)),
        out_shape=jax.ShapeDtypeStruct((t, d), F32),
        compiler_params=_params(("parallel",)),
        name="final_norm",
    )(h, g)


def _rope_tables(seq):
    pos = jnp.arange(seq, dtype=F32)
    inv_freq = 1.0 / (ROPE_THETA ** (jnp.arange(0, HEAD_DIM, 2, dtype=F32) / HEAD_DIM))
    ang = pos[:, None] * inv_freq[None, :]
    cos, sin = jnp.cos(ang), jnp.sin(ang)
    return (jnp.concatenate([cos, cos], axis=-1),
            jnp.concatenate([-sin, sin], axis=-1))


def kernel(x, attn_norm_g, w_qkv, w_o_attn, lambda_q1, lambda_k1, lambda_q2, lambda_k2,
           subln_g, conv_norm_g, w_bch, conv_w, w_o_conv, ffn_norm_g, w_gate, w_up,
           w_down, final_norm_g):
    batch, seq, d = x.shape
    rope_a, rope_b = _rope_tables(seq)
    row = lambda a: a.reshape(1, -1)
    h = x.reshape(batch * seq, d)
    for i in range(DEPTH):
        j = i // 2
        if i % 2 == 0:
            lambda_init = 0.8 - 0.6 * math.exp(-0.3 * i)
            g = row(attn_norm_g[j])
            qv = _qv_proj(h, g, w_qkv[j].astype(BF16), rope_a, rope_b, seq=seq)
            w_k_t = w_qkv[j][:, QK_WIDTH:2 * QK_WIDTH].T.astype(BF16)
            kt = _kt_proj(h, g, w_k_t, rope_a.T, rope_b.T, seq=seq)
            o = _diff_attention(qv, kt, lambda_q1[j], lambda_k1[j], lambda_q2[j],
                                lambda_k2[j], subln_g[j], batch=batch, seq=seq,
                                lambda_init=lambda_init)
            h = _proj_residual(o, w_o_attn[j].astype(BF16), h)
        else:
            gate_b, z = _bch_proj(h, row(conv_norm_g[j]), w_bch[j].astype(BF16))
            h = _conv_out_proj(gate_b, z, conv_w[j], w_o_conv[j].astype(BF16), h, seq=seq)
        hid = _ffn_in(h, row(ffn_norm_g[i]), w_gate[i].astype(BF16), w_up[i].astype(BF16))
        h = _proj_residual(hid, w_down[i].astype(BF16), h)
    return _final_norm(h, row(final_norm_g)).reshape(batch, seq, d)
```

```python
import functools
import math

import jax
import jax.numpy as jnp
from jax import lax
from jax.experimental import pallas as pl
from jax.experimental.pallas import tpu as pltpu

D_MODEL = 2048
DEPTH = 4
N_DIFF_HEADS = 8
HEAD_DIM = 128
V_HEAD_DIM = 2 * HEAD_DIM
QK_WIDTH = 2 * N_DIFF_HEADS * HEAD_DIM
ROPE_THETA = 10000.0
CONV_WIDTH = 3
RMS_EPS = 1e-6
SUBLN_EPS = 1e-5

V7X_LANES = 128
V7X_F32_SUBLANES = 8
V7X_VMEM_BYTES = 64 * 1024 * 1024
VMEM_LIMIT_BYTES = V7X_VMEM_BYTES - 6 * 1024 * 1024

MASK_VALUE = -1e30
LOG2_E = math.log2(math.e)

F32 = jnp.float32
BF16 = jnp.bfloat16


def _params(semantics):
    return pltpu.CompilerParams(dimension_semantics=semantics,
                                vmem_limit_bytes=VMEM_LIMIT_BYTES)


def _run_if(cond, fn):
    def body(_, carry):
        fn()
        return carry
    lax.fori_loop(0, cond.astype(jnp.int32), body, 0)


def _rms_norm_rows(x, g, eps):
    ms = jnp.mean(x * x, axis=-1, keepdims=True)
    return x * lax.rsqrt(ms + eps) * g


def _qv_kernel(xn_ref, w_ref, ra_ref, rb_ref, o_ref, *, tn):
    j = pl.program_id(1)
    n_q_tiles = QK_WIDTH // tn
    acc = jnp.dot(xn_ref[...], w_ref[...], preferred_element_type=F32)

    @pl.when(j < n_q_tiles)
    def _():
        scale = LOG2_E * HEAD_DIM ** -0.5
        ra = ra_ref[...] * scale
        rb = rb_ref[...] * scale
        for c in range(tn // HEAD_DIM):
            t = acc[:, c * HEAD_DIM:(c + 1) * HEAD_DIM]
            r = t * ra + pltpu.roll(t, HEAD_DIM // 2, axis=1) * rb
            o_ref[:, c * HEAD_DIM:(c + 1) * HEAD_DIM] = r.astype(BF16)

    @pl.when(j >= n_q_tiles)
    def _():
        o_ref[...] = acc.astype(BF16)


def _qv_proj(xn, w, layer, ra, rb, *, seq, tm=1024, tn=1024):
    t, d = xn.shape
    seq_tiles = seq // tm
    n_q_tiles = QK_WIDTH // tn
    w_col = lambda j: jnp.where(j < n_q_tiles, j, j + n_q_tiles)
    return pl.pallas_call(
        functools.partial(_qv_kernel, tn=tn),
        grid=(t // tm, 2 * n_q_tiles),
        in_specs=[
            pl.BlockSpec((tm, d), lambda i, j: (i, 0)),
            pl.BlockSpec((None, d, tn), lambda i, j: (layer, 0, w_col(j))),
            pl.BlockSpec((tm, HEAD_DIM), lambda i, j: (i % seq_tiles, 0)),
            pl.BlockSpec((tm, HEAD_DIM), lambda i, j: (i % seq_tiles, 0)),
        ],
        out_specs=pl.BlockSpec((tm, tn), lambda i, j: (i, j)),
        out_shape=jax.ShapeDtypeStruct((t, 2 * QK_WIDTH), BF16),
        compiler_params=_params(("parallel", "parallel")),
        name="qv_proj",
    )(xn, w, ra, rb)


def _kt_kernel(xn_ref, wt_ref, rat_ref, rbt_ref, o_ref, *, tn):
    acc = lax.dot_general(wt_ref[...], xn_ref[...], (((1,), (1,)), ((), ())),
                          preferred_element_type=F32)
    rat = rat_ref[...]
    rbt = rbt_ref[...]
    half = HEAD_DIM // 2
    for c in range(tn // HEAD_DIM):
        t = acc[c * HEAD_DIM:(c + 1) * HEAD_DIM, :]
        swapped = jnp.concatenate([t[half:, :], t[:half, :]], axis=0)
        o_ref[c * HEAD_DIM:(c + 1) * HEAD_DIM, :] = (t * rat + swapped * rbt).astype(BF16)


def _kt_proj(xn, wt, layer, rat, rbt, *, seq, tm=1024, tn=1024):
    t, d = xn.shape
    n = wt.shape[1]
    seq_tiles = seq // tm
    return pl.pallas_call(
        functools.partial(_kt_kernel, tn=tn),
        grid=(t // tm, n // tn),
        in_specs=[
            pl.BlockSpec((tm, d), lambda i, j: (i, 0)),
            pl.BlockSpec((None, tn, d), lambda i, j: (layer, j, 0)),
            pl.BlockSpec((HEAD_DIM, tm), lambda i, j: (0, i % seq_tiles)),
            pl.BlockSpec((HEAD_DIM, tm), lambda i, j: (0, i % seq_tiles)),
        ],
        out_specs=pl.BlockSpec((tn, tm), lambda i, j: (j, i)),
        out_shape=jax.ShapeDtypeStruct((n, t), BF16),
        compiler_params=_params(("parallel", "parallel")),
        name="kt_proj",
    )(xn, wt, rat, rbt)


def _attn_kernel(q_ref, kt_ref, v_ref, lq1_ref, lk1_ref, lq2_ref, lk2_ref, sg_ref,
                 o_ref, s0_ref, s1_ref, p0_ref, p1_ref, a0_ref, a1_ref,
                 m_ref, l_ref, acc_ref, *, tq, tk, lambda_init):
    qi = pl.program_id(2)
    lane_blocks = tk // V7X_LANES
    s_refs, p_refs, alpha_refs = (s0_ref, s1_ref), (p0_ref, p1_ref), (a0_ref, a1_ref)

    m_ref[...] = jnp.full(m_ref.shape, MASK_VALUE, F32)
    l_ref[...] = jnp.zeros(l_ref.shape, F32)
    acc_ref[...] = jnp.zeros(acc_ref.shape, F32)

    def kv_start(kj):
        return pl.multiple_of(kj * tk, tk)

    def scores(kj, slot):
        for c in range(2):
            rows = slice(c * HEAD_DIM, (c + 1) * HEAD_DIM)
            s_refs[slot][c] = jnp.dot(q_ref[:, rows],
                                      kt_ref[rows, pl.ds(kv_start(kj), tk)],
                                      preferred_element_type=F32)

    def softmax(slot, masked):
        for c in range(2):
            s = s_refs[slot][c]
            if masked:
                row = lax.broadcasted_iota(jnp.int32, (tq, tk), 0)
                col = lax.broadcasted_iota(jnp.int32, (tq, tk), 1)
                s = jnp.where(col <= row, s, MASK_VALUE)
            blocks = [s[:, b * V7X_LANES:(b + 1) * V7X_LANES] for b in range(lane_blocks)]
            m_prev = m_ref[c]
            lane_max = functools.reduce(jnp.maximum, blocks)
            m_new = jnp.maximum(m_prev, jnp.max(lane_max, axis=-1, keepdims=True))
            alpha = jnp.exp2(m_prev - m_new)
            ps = [jnp.exp2(blk - m_new) for blk in blocks]
            l_ref[c] = alpha * l_ref[c] + functools.reduce(jnp.add, ps)
            m_ref[c] = m_new
            alpha_refs[slot][c] = alpha
            p_refs[slot][c] = jnp.concatenate(ps, axis=1).astype(BF16)

    def accumulate(kj, slot):
        v = v_ref[pl.ds(kv_start(kj), tk), :]
        for c in range(2):
            alpha = alpha_refs[slot][c]
            pv = jnp.dot(p_refs[slot][c], v, preferred_element_type=F32)
            acc_ref[c] = jnp.concatenate([alpha, alpha], axis=1) * acc_ref[c] + pv

    def pipeline_step(kj, slot):
        scores(kj + 2, slot)
        softmax(1 - slot, masked=False)
        accumulate(kj, slot)

    scores(0, 0)

    def fill():
        scores(1, 1)
        softmax(0, masked=False)

    _run_if(qi >= 1, fill)

    steady_steps = qi - 1

    def pair(u, carry):
        pipeline_step(2 * u, 0)
        pipeline_step(2 * u + 1, 1)
        return carry

    lax.fori_loop(0, steady_steps // 2, pair, 0)

    qi_even = qi % 2 == 0
    _run_if(jnp.logical_and(qi >= 2, qi_even), lambda: pipeline_step(qi - 2, 0))

    def drain(slot):
        softmax(slot, masked=True)
        _run_if(qi >= 1, lambda: accumulate(qi - 1, 1 - slot))
        accumulate(qi, slot)

    _run_if(qi_even, lambda: drain(0))
    _run_if(jnp.logical_not(qi_even), lambda: drain(1))

    lam = (jnp.exp(jnp.sum(lq1_ref[...] * lk1_ref[...], axis=-1, keepdims=True))
           - jnp.exp(jnp.sum(lq2_ref[...] * lk2_ref[...], axis=-1, keepdims=True))
           + lambda_init)
    o1 = acc_ref[0] / jnp.sum(l_ref[0], axis=-1, keepdims=True)
    o2 = acc_ref[1] / jnp.sum(l_ref[1], axis=-1, keepdims=True)
    o = o1 - lam * o2
    o = _rms_norm_rows(o, sg_ref[...], SUBLN_EPS) * (1.0 - lambda_init)
    o_ref[...] = o.astype(BF16)


def _diff_attention(qv, kt, lq1, lk1, lq2, lk2, subln_g, *, batch, seq, lambda_init,
                    tq=512):
    t = qv.shape[0]
    tk = tq
    q_tiles = seq // tq
    vec = lambda a: a.reshape(1, -1)
    small = lambda width: pl.BlockSpec((1, width), lambda b, h, i: (0, 0))
    return pl.pallas_call(
        functools.partial(_attn_kernel, tq=tq, tk=tk, lambda_init=lambda_init),
        grid=(batch, N_DIFF_HEADS, q_tiles),
        in_specs=[
            pl.BlockSpec((tq, V_HEAD_DIM), lambda b, h, i: (b * q_tiles + i, h)),
            pl.BlockSpec((V_HEAD_DIM, seq), lambda b, h, i: (h, b)),
            pl.BlockSpec((seq, V_HEAD_DIM), lambda b, h, i: (b, N_DIFF_HEADS + h)),
            small(HEAD_DIM), small(HEAD_DIM), small(HEAD_DIM), small(HEAD_DIM),
            small(V_HEAD_DIM),
        ],
        out_specs=pl.BlockSpec((tq, V_HEAD_DIM), lambda b, h, i: (b * q_tiles + i, h)),
        out_shape=jax.ShapeDtypeStruct((t, N_DIFF_HEADS * V_HEAD_DIM), BF16),
        scratch_shapes=[
            pltpu.VMEM((2, tq, tk), F32), pltpu.VMEM((2, tq, tk), F32),
            pltpu.VMEM((2, tq, tk), BF16), pltpu.VMEM((2, tq, tk), BF16),
            pltpu.VMEM((2, tq, V7X_LANES), F32), pltpu.VMEM((2, tq, V7X_LANES), F32),
            pltpu.VMEM((2, tq, V7X_LANES), F32),
            pltpu.VMEM((2, tq, V7X_LANES), F32),
            pltpu.VMEM((2, tq, V_HEAD_DIM), F32),
        ],
        compiler_params=_params(("parallel", "parallel", "arbitrary")),
        name="diff_attention",
    )(qv, kt, qv, vec(lq1), vec(lk1), vec(lq2), vec(lk2), vec(subln_g))


def _emit_residual(h, g_ref, out_refs, final):
    if final:
        out_refs[0][...] = _rms_norm_rows(h, g_ref[...], RMS_EPS)
    else:
        out_refs[0][...] = h
        out_refs[1][...] = _rms_norm_rows(h, g_ref[...], RMS_EPS).astype(BF16)


def _residual_out(t, d, tm, final):
    spec = pl.BlockSpec((tm, d), lambda i: (i, 0))
    if final:
        return [spec], [jax.ShapeDtypeStruct((t, d), F32)]
    return [spec, spec], [jax.ShapeDtypeStruct((t, d), F32), jax.ShapeDtypeStruct((t, d), BF16)]


def _proj_residual_kernel(a_ref, w_ref, r_ref, g_ref, *out_refs, final):
    h = r_ref[...] + jnp.dot(a_ref[...], w_ref[...], preferred_element_type=F32)
    _emit_residual(h, g_ref, out_refs, final)


def _proj_residual(a, w, layer, res, g_next, *, final=False, tm=512):
    t, k = a.shape
    n = w.shape[2]
    out_specs, out_shape = _residual_out(t, n, tm, final)
    return pl.pallas_call(
        functools.partial(_proj_residual_kernel, final=final),
        grid=(t // tm,),
        in_specs=[
            pl.BlockSpec((tm, k), lambda i: (i, 0)),
            pl.BlockSpec((None, k, n), lambda i: (layer, 0, 0), pipeline_mode=pl.Buffered(1)),
            pl.BlockSpec((tm, n), lambda i: (i, 0)),
            pl.BlockSpec((1, n), lambda i: (0, 0)),
        ],
        out_specs=out_specs,
        out_shape=out_shape,
        compiler_params=_params(("parallel",)),
        name="proj_residual",
    )(a, w, res, g_next)


def _bch_kernel(xn_ref, wb_ref, wc_ref, wu_ref, gb_ref, z_ref):
    xn = xn_ref[...]
    gb_ref[...] = jnp.dot(xn, wb_ref[...], preferred_element_type=F32).astype(BF16)
    gate_c = jnp.dot(xn, wc_ref[...], preferred_element_type=F32)
    u = jnp.dot(xn, wu_ref[...], preferred_element_type=F32)
    z_ref[...] = (gate_c * u).astype(BF16)


def _bch_proj(xn, w, layer, *, tm=1024, tn=512):
    t, d = xn.shape
    col_tiles = d // tn
    w_spec = lambda part: pl.BlockSpec((None, d, tn),
                                       lambda i, j: (layer, 0, part * col_tiles + j))
    out_spec = pl.BlockSpec((tm, tn), lambda i, j: (i, j))
    return pl.pallas_call(
        _bch_kernel,
        grid=(t // tm, col_tiles),
        in_specs=[pl.BlockSpec((tm, d), lambda i, j: (i, 0)), w_spec(0), w_spec(1), w_spec(2)],
        out_specs=[out_spec, out_spec],
        out_shape=[jax.ShapeDtypeStruct((t, d), BF16)] * 2,
        compiler_params=_params(("parallel", "parallel")),
        name="bch_proj",
    )(xn, w, w, w)


def _conv_out_kernel(gb_ref, z_ref, halo_ref, cw_ref, w_ref, r_ref, g_ref, *out_refs,
                     seq_tiles, final):
    tm = z_ref.shape[0]
    z = z_ref[...].astype(F32)
    halo = jnp.where(pl.program_id(0) % seq_tiles == 0, 0.0, halo_ref[...].astype(F32))
    zext = jnp.concatenate([halo, z], axis=0)
    pad = halo.shape[0]
    cw = cw_ref[...]
    zc = cw[2:3, :] * z
    zc = zc + cw[1:2, :] * zext[pad - 1:pad - 1 + tm, :]
    zc = zc + cw[0:1, :] * zext[pad - 2:pad - 2 + tm, :]
    a = (gb_ref[...].astype(F32) * zc).astype(BF16)
    h = r_ref[...] + jnp.dot(a, w_ref[...], preferred_element_type=F32)
    _emit_residual(h, g_ref, out_refs, final)


def _conv_out_proj(gate_b, z, conv_w, w, layer, res, g_next, *, seq, final=False, tm=512):
    t, d = z.shape
    n = w.shape[2]
    halo_rows = 2 * V7X_F32_SUBLANES
    halo_per_tile = tm // halo_rows
    row_spec = pl.BlockSpec((tm, d), lambda i: (i, 0))
    out_specs, out_shape = _residual_out(t, n, tm, final)
    return pl.pallas_call(
        functools.partial(_conv_out_kernel, seq_tiles=seq // tm, final=final),
        grid=(t // tm,),
        in_specs=[
            row_spec,
            row_spec,
            pl.BlockSpec((halo_rows, d), lambda i: (jnp.maximum(i * halo_per_tile - 1, 0), 0)),
            pl.BlockSpec((None, CONV_WIDTH, d), lambda i: (layer, 0, 0)),
            pl.BlockSpec((None, d, n), lambda i: (layer, 0, 0), pipeline_mode=pl.Buffered(1)),
            pl.BlockSpec((tm, n), lambda i: (i, 0)),
            pl.BlockSpec((1, n), lambda i: (0, 0)),
        ],
        out_specs=out_specs,
        out_shape=out_shape,
        compiler_params=_params(("parallel",)),
        name="conv_out_proj",
    )(gate_b, z, z, conv_w, w, res, g_next)


def _ffn_in_kernel(xn_ref, wg_ref, wu_ref, o_ref):
    xn = xn_ref[...]
    gate = jnp.dot(xn, wg_ref[...], preferred_element_type=F32)
    up = jnp.dot(xn, wu_ref[...], preferred_element_type=F32)
    o_ref[...] = (gate * jax.nn.sigmoid(gate) * up).astype(BF16)


def _ffn_in(xn, w_gate, w_up, layer, *, tm=1024, tf=512):
    t, d = xn.shape
    f = w_gate.shape[2]
    w_spec = pl.BlockSpec((None, d, tf), lambda i, j: (layer, 0, j))
    return pl.pallas_call(
        _ffn_in_kernel,
        grid=(t // tm, f // tf),
        in_specs=[pl.BlockSpec((tm, d), lambda i, j: (i, 0)), w_spec, w_spec],
        out_specs=pl.BlockSpec((tm, tf), lambda i, j: (i, j)),
        out_shape=jax.ShapeDtypeStruct((t, f), BF16),
        compiler_params=_params(("parallel", "parallel")),
        name="ffn_in",
    )(xn, w_gate, w_up)


def _first_norm_kernel(x_ref, g_ref, o_ref):
    o_ref[...] = _rms_norm_rows(x_ref[...], g_ref[...], RMS_EPS).astype(BF16)


def _first_norm(h, g, *, tm=1024):
    t, d = h.shape
    return pl.pallas_call(
        _first_norm_kernel,
        grid=(t // tm,),
        in_specs=[pl.BlockSpec((tm, d), lambda i: (i, 0)),
                  pl.BlockSpec((1, d), lambda i: (0, 0))],
        out_specs=pl.BlockSpec((tm, d), lambda i: (i, 0)),
        out_shape=jax.ShapeDtypeStruct((t, d), BF16),
        compiler_params=_params(("parallel",)),
        name="first_norm",
    )(h, g)


def _rope_tables(seq):
    pos = jnp.arange(seq, dtype=F32)
    inv_freq = 1.0 / (ROPE_THETA ** (jnp.arange(0, HEAD_DIM, 2, dtype=F32) / HEAD_DIM))
    ang = pos[:, None] * inv_freq[None, :]
    cos, sin = jnp.cos(ang), jnp.sin(ang)
    return (jnp.concatenate([cos, cos], axis=-1),
            jnp.concatenate([-sin, sin], axis=-1))


def kernel(x, attn_norm_g, w_qkv, w_o_attn, lambda_q1, lambda_k1, lambda_q2, lambda_k2,
           subln_g, conv_norm_g, w_bch, conv_w, w_o_conv, ffn_norm_g, w_gate, w_up,
           w_down, final_norm_g):
    batch, seq, d = x.shape
    rope_a, rope_b = _rope_tables(seq)
    rope_a_t, rope_b_t = rope_a.T, rope_b.T
    row = lambda a: a.reshape(1, -1)
    w_qkv_b, w_o_attn_b, w_bch_b, w_o_conv_b = (
        a.astype(BF16) for a in (w_qkv, w_o_attn, w_bch, w_o_conv))
    w_gate_b, w_up_b, w_down_b = (a.astype(BF16) for a in (w_gate, w_up, w_down))
    w_k_t = jnp.swapaxes(w_qkv[:, :, QK_WIDTH:2 * QK_WIDTH], 1, 2).astype(BF16)

    def mixer_norm_g(layer):
        j = layer // 2
        return row(attn_norm_g[j] if layer % 2 == 0 else conv_norm_g[j])

    h = x.reshape(batch * seq, d)
    xn = _first_norm(h, mixer_norm_g(0))
    for i in range(DEPTH):
        j = i // 2
        g_ffn = row(ffn_norm_g[i])
        if i % 2 == 0:
            lambda_init = 0.8 - 0.6 * math.exp(-0.3 * i)
            qv = _qv_proj(xn, w_qkv_b, j, rope_a, rope_b, seq=seq)
            kt = _kt_proj(xn, w_k_t, j, rope_a_t, rope_b_t, seq=seq)
            o = _diff_attention(qv, kt, lambda_q1[j], lambda_k1[j], lambda_q2[j],
                                lambda_k2[j], subln_g[j], batch=batch, seq=seq,
                                lambda_init=lambda_init)
            h, xn = _proj_residual(o, w_o_attn_b, j, h, g_ffn)
        else:
            gate_b, z = _bch_proj(xn, w_bch_b, j)
            h, xn = _conv_out_proj(gate_b, z, conv_w, w_o_conv_b, j, h, g_ffn, seq=seq)
        hid = _ffn_in(xn, w_gate_b, w_up_b, i)
        if i + 1 < DEPTH:
            h, xn = _proj_residual(hid, w_down_b, i, h, mixer_norm_g(i + 1))
        else:
            (out,) = _proj_residual(hid, w_down_b, i, h, row(final_norm_g), final=True)
    return out.reshape(batch, seq, d)
```

```python
import functools
import math

import jax
import jax.numpy as jnp
from jax import lax
from jax.experimental import pallas as pl
from jax.experimental.pallas import tpu as pltpu

D_MODEL = 2048
DEPTH = 4
N_DIFF_HEADS = 8
HEAD_DIM = 128
V_HEAD_DIM = 2 * HEAD_DIM
QK_WIDTH = 2 * N_DIFF_HEADS * HEAD_DIM
ROPE_THETA = 10000.0
CONV_WIDTH = 3
RMS_EPS = 1e-6
SUBLN_EPS = 1e-5

V7X_LANES = 128
V7X_F32_SUBLANES = 8
V7X_VMEM_BYTES = 64 * 1024 * 1024
VMEM_LIMIT_BYTES = V7X_VMEM_BYTES - 6 * 1024 * 1024

MASK_VALUE = -1e30
LOG2_E = math.log2(math.e)

F32 = jnp.float32
BF16 = jnp.bfloat16


def _params(semantics):
    return pltpu.CompilerParams(dimension_semantics=semantics,
                                vmem_limit_bytes=VMEM_LIMIT_BYTES)


def _run_if(cond, fn):
    def body(_, carry):
        fn()
        return carry
    lax.fori_loop(0, cond.astype(jnp.int32), body, 0)


def _rms_norm_rows(x, g, eps):
    ms = jnp.mean(x * x, axis=-1, keepdims=True)
    return x * lax.rsqrt(ms + eps) * g


def _k_kernel(xn_ref, w_ref, ra_ref, rb_ref, o_ref, *, tn):
    acc = jnp.dot(xn_ref[...], w_ref[...], preferred_element_type=F32)
    ra = ra_ref[...]
    rb = rb_ref[...]
    for c in range(tn // HEAD_DIM):
        t = acc[:, c * HEAD_DIM:(c + 1) * HEAD_DIM]
        r = t * ra + pltpu.roll(t, HEAD_DIM // 2, axis=1) * rb
        o_ref[:, c * HEAD_DIM:(c + 1) * HEAD_DIM] = r.astype(BF16)


def _k_proj(xn, w, layer, ra, rb, *, seq, tm=1024, tn=1024):
    t, d = xn.shape
    seq_tiles = seq // tm
    k_tiles = QK_WIDTH // tn
    return pl.pallas_call(
        functools.partial(_k_kernel, tn=tn),
        grid=(t // tm, k_tiles),
        in_specs=[
            pl.BlockSpec((tm, d), lambda i, j: (i, 0)),
            pl.BlockSpec((None, d, tn), lambda i, j: (layer, 0, k_tiles + j)),
            pl.BlockSpec((tm, HEAD_DIM), lambda i, j: (i % seq_tiles, 0)),
            pl.BlockSpec((tm, HEAD_DIM), lambda i, j: (i % seq_tiles, 0)),
        ],
        out_specs=pl.BlockSpec((tm, tn), lambda i, j: (i, j)),
        out_shape=jax.ShapeDtypeStruct((t, QK_WIDTH), BF16),
        compiler_params=_params(("parallel", "parallel")),
        name="k_proj",
    )(xn, w, ra, rb)


def _qvt_kernel(xn_ref, wt_ref, rat_ref, rbt_ref, o_ref, *, tn):
    j = pl.program_id(1)
    n_q_tiles = QK_WIDTH // tn
    acc = lax.dot_general(wt_ref[...], xn_ref[...], (((1,), (1,)), ((), ())),
                          preferred_element_type=F32)

    @pl.when(j < n_q_tiles)
    def _():
        scale = LOG2_E * HEAD_DIM ** -0.5
        rat = rat_ref[...] * scale
        rbt = rbt_ref[...] * scale
        half = HEAD_DIM // 2
        for c in range(tn // HEAD_DIM):
            t = acc[c * HEAD_DIM:(c + 1) * HEAD_DIM, :]
            swapped = jnp.concatenate([t[half:, :], t[:half, :]], axis=0)
            o_ref[c * HEAD_DIM:(c + 1) * HEAD_DIM, :] = (t * rat + swapped * rbt).astype(BF16)

    @pl.when(j >= n_q_tiles)
    def _():
        o_ref[...] = acc.astype(BF16)


def _qvt_proj(xn, wt, layer, rat, rbt, *, seq, tm=1024, tn=1024):
    t, d = xn.shape
    n = wt.shape[1]
    seq_tiles = seq // tm
    return pl.pallas_call(
        functools.partial(_qvt_kernel, tn=tn),
        grid=(t // tm, n // tn),
        in_specs=[
            pl.BlockSpec((tm, d), lambda i, j: (i, 0)),
            pl.BlockSpec((None, tn, d), lambda i, j: (layer, j, 0)),
            pl.BlockSpec((HEAD_DIM, tm), lambda i, j: (0, i % seq_tiles)),
            pl.BlockSpec((HEAD_DIM, tm), lambda i, j: (0, i % seq_tiles)),
        ],
        out_specs=pl.BlockSpec((tn, tm), lambda i, j: (j, i)),
        out_shape=jax.ShapeDtypeStruct((n, t), BF16),
        compiler_params=_params(("parallel", "parallel")),
        name="qvt_proj",
    )(xn, wt, rat, rbt)


def _attn_kernel(qt_ref, k_ref, vt_ref, lq1_ref, lk1_ref, lq2_ref, lk2_ref, sg_ref,
                 o_ref, s0_ref, s1_ref, c0_ref, c1_ref, p0_ref, p1_ref, p2_ref,
                 a0_ref, a1_ref, a2_ref, m_ref, l_ref, acc_ref, *, tq, tk, lambda_init):
    qi = pl.program_id(2)
    s_refs = (s0_ref, s1_ref)
    p_refs, alpha_refs = (p0_ref, p1_ref, p2_ref), (a0_ref, a1_ref, a2_ref)
    chunk_max_refs = (c0_ref, c1_ref)

    m_ref[...] = jnp.full(m_ref.shape, MASK_VALUE, F32)
    l_ref[...] = jnp.zeros(l_ref.shape, F32)
    acc_ref[...] = jnp.zeros(acc_ref.shape, F32)

    def kv_start(kj):
        return pl.multiple_of(kj * tk, tk)

    def scores(kj, slot):
        for c in range(2):
            part = slice(c * HEAD_DIM, (c + 1) * HEAD_DIM)
            s = jnp.dot(k_ref[pl.ds(kv_start(kj), tk), part], qt_ref[part, :],
                        preferred_element_type=F32)
            s_refs[slot][c] = s
            chunk_max_refs[slot][c] = jnp.max(s, axis=0, keepdims=True)

    def softmax(slot, masked, out_slot=None):
        out_slot = slot if out_slot is None else out_slot
        for c in range(2):
            s = s_refs[slot][c]
            if masked:
                key = lax.broadcasted_iota(jnp.int32, (tk, tq), 0)
                query = lax.broadcasted_iota(jnp.int32, (tk, tq), 1)
                s = jnp.where(key <= query, s, MASK_VALUE)
                chunk_max = jnp.max(s, axis=0, keepdims=True)
            else:
                chunk_max = chunk_max_refs[slot][c]
            m_prev = m_ref[c]
            m_new = jnp.maximum(m_prev, chunk_max)
            alpha = jnp.exp2(m_prev - m_new)
            p = jnp.exp2(s - m_new)
            l_ref[c] = alpha * l_ref[c] + jnp.sum(p, axis=0, keepdims=True)
            m_ref[c] = m_new
            alpha_refs[out_slot][c] = alpha
            p_refs[out_slot][c] = p.astype(BF16)

    def accumulate(kj, slot):
        vt = vt_ref[:, pl.ds(kv_start(kj), tk)]
        for c in range(2):
            pv = jnp.dot(vt, p_refs[slot][c], preferred_element_type=F32)
            acc_ref[c] = alpha_refs[slot][c] * acc_ref[c] + pv

    def pipeline_step(kj, slot):
        scores(kj + 2, slot)
        softmax(1 - slot, masked=False)
        accumulate(kj, slot)

    scores(0, 0)

    def fill():
        scores(1, 1)
        softmax(0, masked=False)

    _run_if(qi >= 1, fill)

    steady_steps = qi - 1

    def pair(u, carry):
        pipeline_step(2 * u, 0)
        pipeline_step(2 * u + 1, 1)
        return carry

    lax.fori_loop(0, steady_steps // 2, pair, 0)

    qi_even = qi % 2 == 0

    def drain_even():
        pipeline_step(qi - 2, 0)
        softmax(0, masked=True, out_slot=2)
        accumulate(qi - 1, 1)
        accumulate(qi, 2)

    def drain_odd():
        softmax(1, masked=True)
        accumulate(qi - 1, 0)
        accumulate(qi, 1)

    def drain_first():
        softmax(0, masked=True)
        accumulate(0, 0)

    _run_if(jnp.logical_and(qi >= 2, qi_even), drain_even)
    _run_if(jnp.logical_not(qi_even), drain_odd)
    _run_if(qi == 0, drain_first)

    lam = (jnp.exp(jnp.sum(lq1_ref[...] * lk1_ref[...], axis=-1, keepdims=True))
           - jnp.exp(jnp.sum(lq2_ref[...] * lk2_ref[...], axis=-1, keepdims=True))
           + lambda_init)
    ot = acc_ref[0] / l_ref[0] - lam * (acc_ref[1] / l_ref[1])
    ms = jnp.mean(ot * ot, axis=0, keepdims=True)
    ot = ot * lax.rsqrt(ms + SUBLN_EPS)
    gain = sg_ref[...] * (1.0 - lambda_init)
    o_ref[...] = (ot.T * gain).astype(BF16)


def _diff_attention(qvt, k, lq1, lk1, lq2, lk2, subln_g, *, batch, seq, lambda_init,
                    tq=512):
    t = k.shape[0]
    tk = tq
    q_tiles = seq // tq
    vec = lambda a: a.reshape(1, -1)
    small = lambda width: pl.BlockSpec((1, width), lambda b, h, i: (0, 0))
    return pl.pallas_call(
        functools.partial(_attn_kernel, tq=tq, tk=tk, lambda_init=lambda_init),
        grid=(batch, N_DIFF_HEADS, q_tiles),
        in_specs=[
            pl.BlockSpec((V_HEAD_DIM, tq), lambda b, h, i: (h, b * q_tiles + i)),
            pl.BlockSpec((seq, V_HEAD_DIM), lambda b, h, i: (b, h)),
            pl.BlockSpec((V_HEAD_DIM, seq), lambda b, h, i: (N_DIFF_HEADS + h, b)),
            small(HEAD_DIM), small(HEAD_DIM), small(HEAD_DIM), small(HEAD_DIM),
            small(V_HEAD_DIM),
        ],
        out_specs=pl.BlockSpec((tq, V_HEAD_DIM), lambda b, h, i: (b * q_tiles + i, h)),
        out_shape=jax.ShapeDtypeStruct((t, N_DIFF_HEADS * V_HEAD_DIM), BF16),
        scratch_shapes=[
            pltpu.VMEM((2, tk, tq), F32), pltpu.VMEM((2, tk, tq), F32),
            pltpu.VMEM((2, 1, tq), F32), pltpu.VMEM((2, 1, tq), F32),
            pltpu.VMEM((2, tk, tq), BF16), pltpu.VMEM((2, tk, tq), BF16),
            pltpu.VMEM((2, tk, tq), BF16),
            pltpu.VMEM((2, 1, tq), F32), pltpu.VMEM((2, 1, tq), F32),
            pltpu.VMEM((2, 1, tq), F32),
            pltpu.VMEM((2, 1, tq), F32),
            pltpu.VMEM((2, 1, tq), F32),
            pltpu.VMEM((2, V_HEAD_DIM, tq), F32),
        ],
        compiler_params=_params(("parallel", "parallel", "arbitrary")),
        name="diff_attention",
    )(qvt, k, qvt, vec(lq1), vec(lk1), vec(lq2), vec(lk2), vec(subln_g))


def _emit_residual(h, g_ref, out_refs, final):
    if final:
        out_refs[0][...] = _rms_norm_rows(h, g_ref[...], RMS_EPS)
    else:
        out_refs[0][...] = h
        out_refs[1][...] = _rms_norm_rows(h, g_ref[...], RMS_EPS).astype(BF16)


def _residual_out(t, d, tm, final):
    spec = pl.BlockSpec((tm, d), lambda i: (i, 0))
    if final:
        return [spec], [jax.ShapeDtypeStruct((t, d), F32)]
    return [spec, spec], [jax.ShapeDtypeStruct((t, d), F32), jax.ShapeDtypeStruct((t, d), BF16)]


def _proj_residual_kernel(a_ref, w_ref, r_ref, g_ref, *out_refs, final):
    h = r_ref[...] + jnp.dot(a_ref[...], w_ref[...], preferred_element_type=F32)
    _emit_residual(h, g_ref, out_refs, final)


def _proj_residual(a, w, layer, res, g_next, *, final=False, tm=512):
    t, k = a.shape
    n = w.shape[2]
    out_specs, out_shape = _residual_out(t, n, tm, final)
    return pl.pallas_call(
        functools.partial(_proj_residual_kernel, final=final),
        grid=(t // tm,),
        in_specs=[
            pl.BlockSpec((tm, k), lambda i: (i, 0)),
            pl.BlockSpec((None, k, n), lambda i: (layer, 0, 0), pipeline_mode=pl.Buffered(1)),
            pl.BlockSpec((tm, n), lambda i: (i, 0)),
            pl.BlockSpec((1, n), lambda i: (0, 0)),
        ],
        out_specs=out_specs,
        out_shape=out_shape,
        compiler_params=_params(("parallel",)),
        name="proj_residual",
    )(a, w, res, g_next)


def _bch_kernel(xn_ref, wb_ref, wc_ref, wu_ref, gb_ref, z_ref):
    xn = xn_ref[...]
    gb_ref[...] = jnp.dot(xn, wb_ref[...], preferred_element_type=F32).astype(BF16)
    gate_c = jnp.dot(xn, wc_ref[...], preferred_element_type=F32)
    u = jnp.dot(xn, wu_ref[...], preferred_element_type=F32)
    z_ref[...] = (gate_c * u).astype(BF16)


def _bch_proj(xn, w, layer, *, tm=1024, tn=512):
    t, d = xn.shape
    col_tiles = d // tn
    w_spec = lambda part: pl.BlockSpec((None, d, tn),
                                       lambda i, j: (layer, 0, part * col_tiles + j))
    out_spec = pl.BlockSpec((tm, tn), lambda i, j: (i, j))
    return pl.pallas_call(
        _bch_kernel,
        grid=(t // tm, col_tiles),
        in_specs=[pl.BlockSpec((tm, d), lambda i, j: (i, 0)), w_spec(0), w_spec(1), w_spec(2)],
        out_specs=[out_spec, out_spec],
        out_shape=[jax.ShapeDtypeStruct((t, d), BF16)] * 2,
        compiler_params=_params(("parallel", "parallel")),
        name="bch_proj",
    )(xn, w, w, w)


def _conv_out_kernel(gb_ref, z_ref, halo_ref, cw_ref, w_ref, r_ref, g_ref, *out_refs,
                     seq_tiles, final):
    tm = z_ref.shape[0]
    z = z_ref[...].astype(F32)
    halo = jnp.where(pl.program_id(0) % seq_tiles == 0, 0.0, halo_ref[...].astype(F32))
    zext = jnp.concatenate([halo, z], axis=0)
    pad = halo.shape[0]
    cw = cw_ref[...]
    zc = cw[2:3, :] * z
    zc = zc + cw[1:2, :] * zext[pad - 1:pad - 1 + tm, :]
    zc = zc + cw[0:1, :] * zext[pad - 2:pad - 2 + tm, :]
    a = (gb_ref[...].astype(F32) * zc).astype(BF16)
    h = r_ref[...] + jnp.dot(a, w_ref[...], preferred_element_type=F32)
    _emit_residual(h, g_ref, out_refs, final)


def _conv_out_proj(gate_b, z, conv_w, w, layer, res, g_next, *, seq, final=False, tm=512):
    t, d = z.shape
    n = w.shape[2]
    halo_rows = 2 * V7X_F32_SUBLANES
    halo_per_tile = tm // halo_rows
    row_spec = pl.BlockSpec((tm, d), lambda i: (i, 0))
    out_specs, out_shape = _residual_out(t, n, tm, final)
    return pl.pallas_call(
        functools.partial(_conv_out_kernel, seq_tiles=seq // tm, final=final),
        grid=(t // tm,),
        in_specs=[
            row_spec,
            row_spec,
            pl.BlockSpec((halo_rows, d), lambda i: (jnp.maximum(i * halo_per_tile - 1, 0), 0)),
            pl.BlockSpec((None, CONV_WIDTH, d), lambda i: (layer, 0, 0)),
            pl.BlockSpec((None, d, n), lambda i: (layer, 0, 0), pipeline_mode=pl.Buffered(1)),
            pl.BlockSpec((tm, n), lambda i: (i, 0)),
            pl.BlockSpec((1, n), lambda i: (0, 0)),
        ],
        out_specs=out_specs,
        out_shape=out_shape,
        compiler_params=_params(("parallel",)),
        name="conv_out_proj",
    )(gate_b, z, z, conv_w, w, res, g_next)


def _ffn_in_kernel(xn_ref, wg_ref, wu_ref, o_ref):
    xn = xn_ref[...]
    gate = jnp.dot(xn, wg_ref[...], preferred_element_type=F32)
    up = jnp.dot(xn, wu_ref[...], preferred_element_type=F32)
    o_ref[...] = (gate * jax.nn.sigmoid(gate) * up).astype(BF16)


def _ffn_in(xn, w_gate, w_up, layer, *, tm=1024, tf=512):
    t, d = xn.shape
    f = w_gate.shape[2]
    w_spec = pl.BlockSpec((None, d, tf), lambda i, j: (layer, 0, j))
    return pl.pallas_call(
        _ffn_in_kernel,
        grid=(t // tm, f // tf),
        in_specs=[pl.BlockSpec((tm, d), lambda i, j: (i, 0)), w_spec, w_spec],
        out_specs=pl.BlockSpec((tm, tf), lambda i, j: (i, j)),
        out_shape=jax.ShapeDtypeStruct((t, f), BF16),
        compiler_params=_params(("parallel", "parallel")),
        name="ffn_in",
    )(xn, w_gate, w_up)


def _first_norm_kernel(x_ref, g_ref, o_ref):
    o_ref[...] = _rms_norm_rows(x_ref[...], g_ref[...], RMS_EPS).astype(BF16)


def _first_norm(h, g, *, tm=1024):
    t, d = h.shape
    return pl.pallas_call(
        _first_norm_kernel,
        grid=(t // tm,),
        in_specs=[pl.BlockSpec((tm, d), lambda i: (i, 0)),
                  pl.BlockSpec((1, d), lambda i: (0, 0))],
        out_specs=pl.BlockSpec((tm, d), lambda i: (i, 0)),
        out_shape=jax.ShapeDtypeStruct((t, d), BF16),
        compiler_params=_params(("parallel",)),
        name="first_norm",
    )(h, g)


def _rope_tables(seq):
    pos = jnp.arange(seq, dtype=F32)
    inv_freq = 1.0 / (ROPE_THETA ** (jnp.arange(0, HEAD_DIM, 2, dtype=F32) / HEAD_DIM))
    ang = pos[:, None] * inv_freq[None, :]
    cos, sin = jnp.cos(ang), jnp.sin(ang)
    return (jnp.concatenate([cos, cos], axis=-1),
            jnp.concatenate([-sin, sin], axis=-1))


def kernel(x, attn_norm_g, w_qkv, w_o_attn, lambda_q1, lambda_k1, lambda_q2, lambda_k2,
           subln_g, conv_norm_g, w_bch, conv_w, w_o_conv, ffn_norm_g, w_gate, w_up,
           w_down, final_norm_g):
    batch, seq, d = x.shape
    rope_a, rope_b = _rope_tables(seq)
    rope_a_t, rope_b_t = rope_a.T, rope_b.T
    row = lambda a: a.reshape(1, -1)
    w_qkv_b, w_o_attn_b, w_bch_b, w_o_conv_b = (
        a.astype(BF16) for a in (w_qkv, w_o_attn, w_bch, w_o_conv))
    w_gate_b, w_up_b, w_down_b = (a.astype(BF16) for a in (w_gate, w_up, w_down))
    w_qkv_b = lax.optimization_barrier(w_qkv_b)
    w_qv_t = jnp.swapaxes(
        jnp.concatenate([w_qkv_b[:, :, :QK_WIDTH], w_qkv_b[:, :, 2 * QK_WIDTH:]], axis=2), 1, 2)

    def mixer_norm_g(layer):
        j = layer // 2
        return row(attn_norm_g[j] if layer % 2 == 0 else conv_norm_g[j])

    h = x.reshape(batch * seq, d)
    xn = _first_norm(h, mixer_norm_g(0))
    for i in range(DEPTH):
        j = i // 2
        g_ffn = row(ffn_norm_g[i])
        if i % 2 == 0:
            lambda_init = 0.8 - 0.6 * math.exp(-0.3 * i)
            k = _k_proj(xn, w_qkv_b, j, rope_a, rope_b, seq=seq)
            qvt = _qvt_proj(xn, w_qv_t, j, rope_a_t, rope_b_t, seq=seq)
            o = _diff_attention(qvt, k, lambda_q1[j], lambda_k1[j], lambda_q2[j],
                                lambda_k2[j], subln_g[j], batch=batch, seq=seq,
                                lambda_init=lambda_init)
            h, xn = _proj_residual(o, w_o_attn_b, j, h, g_ffn)
        else:
            gate_b, z = _bch_proj(xn, w_bch_b, j)
            h, xn = _conv_out_proj(gate_b, z, conv_w, w_o_conv_b, j, h, g_ffn, seq=seq)
        hid = _ffn_in(xn, w_gate_b, w_up_b, i)
        if i + 1 < DEPTH:
            h, xn = _proj_residual(hid, w_down_b, i, h, mixer_norm_g(i + 1))
        else:
            (out,) = _proj_residual(hid, w_down_b, i, h, row(final_norm_g), final=True)
    return out.reshape(batch, seq, d)
```

```python
import functools
import math

import jax
import jax.numpy as jnp
from jax import lax
from jax.experimental import pallas as pl
from jax.experimental.pallas import tpu as pltpu

D_MODEL = 2048
DEPTH = 4
N_DIFF_HEADS = 8
HEAD_DIM = 128
V_HEAD_DIM = 2 * HEAD_DIM
QK_WIDTH = 2 * N_DIFF_HEADS * HEAD_DIM
ROPE_THETA = 10000.0
CONV_WIDTH = 3
RMS_EPS = 1e-6
SUBLN_EPS = 1e-5

V7X_LANES = 128
V7X_F32_SUBLANES = 8
V7X_VMEM_BYTES = 64 * 1024 * 1024
VMEM_LIMIT_BYTES = V7X_VMEM_BYTES - 6 * 1024 * 1024

MASK_VALUE = -1e30
LOG2_E = math.log2(math.e)

F32 = jnp.float32
BF16 = jnp.bfloat16


def _params(semantics):
    return pltpu.CompilerParams(dimension_semantics=semantics,
                                vmem_limit_bytes=VMEM_LIMIT_BYTES)


def _run_if(cond, fn):
    def body(_, carry):
        fn()
        return carry
    lax.fori_loop(0, cond.astype(jnp.int32), body, 0)


def _rms_norm_rows(x, g, eps):
    ms = jnp.mean(x * x, axis=-1, keepdims=True)
    return x * lax.rsqrt(ms + eps) * g


def _k_kernel(xn_ref, w_ref, ra_ref, rb_ref, o_ref, *, tn):
    acc = jnp.dot(xn_ref[...], w_ref[...], preferred_element_type=F32)
    ra = ra_ref[...]
    rb = rb_ref[...]
    for c in range(tn // HEAD_DIM):
        t = acc[:, c * HEAD_DIM:(c + 1) * HEAD_DIM]
        r = t * ra + pltpu.roll(t, HEAD_DIM // 2, axis=1) * rb
        o_ref[:, c * HEAD_DIM:(c + 1) * HEAD_DIM] = r.astype(BF16)


def _k_proj(xn, w, layer, ra, rb, *, seq, tm=1024, tn=1024):
    t, d = xn.shape
    seq_tiles = seq // tm
    k_tiles = QK_WIDTH // tn
    return pl.pallas_call(
        functools.partial(_k_kernel, tn=tn),
        grid=(t // tm, k_tiles),
        in_specs=[
            pl.BlockSpec((tm, d), lambda i, j: (i, 0)),
            pl.BlockSpec((None, d, tn), lambda i, j: (layer, 0, k_tiles + j)),
            pl.BlockSpec((tm, HEAD_DIM), lambda i, j: (i % seq_tiles, 0)),
            pl.BlockSpec((tm, HEAD_DIM), lambda i, j: (i % seq_tiles, 0)),
        ],
        out_specs=pl.BlockSpec((tm, tn), lambda i, j: (i, j)),
        out_shape=jax.ShapeDtypeStruct((t, QK_WIDTH), BF16),
        compiler_params=_params(("parallel", "parallel")),
        name="k_proj",
    )(xn, w, ra, rb)


def _qvt_kernel(xn_ref, wt_ref, rat_ref, rbt_ref, o_ref, *, tn):
    j = pl.program_id(1)
    n_q_tiles = QK_WIDTH // tn
    acc = lax.dot_general(wt_ref[...], xn_ref[...], (((1,), (1,)), ((), ())),
                          preferred_element_type=F32)

    @pl.when(j < n_q_tiles)
    def _():
        scale = LOG2_E * HEAD_DIM ** -0.5
        rat = rat_ref[...] * scale
        rbt = rbt_ref[...] * scale
        half = HEAD_DIM // 2
        for c in range(tn // HEAD_DIM):
            t = acc[c * HEAD_DIM:(c + 1) * HEAD_DIM, :]
            swapped = jnp.concatenate([t[half:, :], t[:half, :]], axis=0)
            o_ref[c * HEAD_DIM:(c + 1) * HEAD_DIM, :] = (t * rat + swapped * rbt).astype(BF16)

    @pl.when(j >= n_q_tiles)
    def _():
        o_ref[...] = acc.astype(BF16)


def _qvt_proj(xn, wt, layer, rat, rbt, *, seq, tm=1024, tn=1024):
    t, d = xn.shape
    n = wt.shape[1]
    seq_tiles = seq // tm
    return pl.pallas_call(
        functools.partial(_qvt_kernel, tn=tn),
        grid=(t // tm, n // tn),
        in_specs=[
            pl.BlockSpec((tm, d), lambda i, j: (i, 0)),
            pl.BlockSpec((None, tn, d), lambda i, j: (layer, j, 0)),
            pl.BlockSpec((HEAD_DIM, tm), lambda i, j: (0, i % seq_tiles)),
            pl.BlockSpec((HEAD_DIM, tm), lambda i, j: (0, i % seq_tiles)),
        ],
        out_specs=pl.BlockSpec((tn, tm), lambda i, j: (j, i)),
        out_shape=jax.ShapeDtypeStruct((n, t), BF16),
        compiler_params=_params(("parallel", "parallel")),
        name="qvt_proj",
    )(xn, wt, rat, rbt)


def _attn_kernel(qt_ref, k_ref, vt_ref, lq1_ref, lk1_ref, lq2_ref, lk2_ref, sg_ref,
                 o_ref, s0_ref, s1_ref, c0_ref, c1_ref, p0_ref, p1_ref, a0_ref, a1_ref,
                 m_ref, l_ref, acc_ref, *, tq, tk, heads, lambda_init):
    qi = pl.program_id(2)
    units = [(hh, c) for hh in range(heads) for c in range(2)]
    s_refs = (s0_ref, s1_ref)
    p_refs, alpha_refs = (p0_ref, p1_ref), (a0_ref, a1_ref)
    chunk_max_refs = (c0_ref, c1_ref)

    m_ref[1] = jnp.full(m_ref.shape[1:], MASK_VALUE, F32)
    l_ref[1] = jnp.zeros(l_ref.shape[1:], F32)
    acc_ref[1] = jnp.zeros(acc_ref.shape[1:], F32)

    def kv_start(kj):
        return pl.multiple_of(kj * tk, tk)

    def scores(kj, slot, only=None):
        for u, (hh, c) in enumerate(units):
            if only is not None and hh != only:
                continue
            start = hh * V_HEAD_DIM + c * HEAD_DIM
            part = slice(start, start + HEAD_DIM)
            s = jnp.dot(k_ref[pl.ds(kv_start(kj), tk), part], qt_ref[part, :],
                        preferred_element_type=F32)
            s_refs[slot][u] = s
            chunk_max_refs[slot][u] = jnp.max(s, axis=0, keepdims=True)

    def softmax(slot, masked, only=None):
        prev = 1 - slot
        for c, (hh, _) in enumerate(units):
            if only is not None and hh != only:
                continue
            s = s_refs[slot][c]
            if masked:
                key = lax.broadcasted_iota(jnp.int32, (tk, tq), 0)
                query = lax.broadcasted_iota(jnp.int32, (tk, tq), 1)
                s = jnp.where(key <= query, s, MASK_VALUE)
                chunk_max = jnp.max(s, axis=0, keepdims=True)
            else:
                chunk_max = chunk_max_refs[slot][c]
            m_prev = m_ref[prev, c]
            m_new = jnp.maximum(m_prev, chunk_max)
            alpha = jnp.exp2(m_prev - m_new)
            p = jnp.exp2(s - m_new)
            l_ref[slot, c] = alpha * l_ref[prev, c] + jnp.sum(p, axis=0, keepdims=True)
            m_ref[slot, c] = m_new
            alpha_refs[slot][c] = alpha
            p_refs[slot][c] = p.astype(BF16)

    def accumulate(kj, slot, only=None):
        for u, (hh, _) in enumerate(units):
            if only is not None and hh != only:
                continue
            vt = vt_ref[hh * V_HEAD_DIM:(hh + 1) * V_HEAD_DIM, pl.ds(kv_start(kj), tk)]
            pv = jnp.dot(vt, p_refs[slot][u], preferred_element_type=F32)
            acc_ref[slot, u] = alpha_refs[slot][u] * acc_ref[1 - slot, u] + pv

    def pipeline_step(kj, slot, only=None):
        scores(kj + 2, slot, only)
        softmax(1 - slot, masked=False, only=only)
        accumulate(kj, slot, only)

    scores(0, 0)

    def fill():
        scores(1, 1)
        softmax(0, masked=False)

    _run_if(qi >= 1, fill)

    steady_steps = qi - 1

    def pair(u, carry):
        for hh in range(heads):
            pipeline_step(2 * u, 0, hh)
            pipeline_step(2 * u + 1, 1, hh)
        return carry

    lax.fori_loop(0, steady_steps // 2, pair, 0)

    qi_even = qi % 2 == 0
    qi_odd = jnp.logical_not(qi_even)
    even_with_prev = jnp.logical_and(qi >= 2, qi_even)
    _run_if(even_with_prev, lambda: [pipeline_step(qi - 2, 0, hh) for hh in range(heads)])

    def drain(slot, with_prev=True):
        for hh in range(heads):
            softmax(slot, masked=True, only=hh)
            if with_prev:
                accumulate(qi - 1, 1 - slot, hh)
            accumulate(qi, slot, hh)

    _run_if(even_with_prev, lambda: drain(0))
    _run_if(qi_odd, lambda: drain(1))
    _run_if(qi == 0, lambda: drain(0, with_prev=False))

    lam = (jnp.exp(jnp.sum(lq1_ref[...] * lk1_ref[...], axis=-1, keepdims=True))
           - jnp.exp(jnp.sum(lq2_ref[...] * lk2_ref[...], axis=-1, keepdims=True))
           + lambda_init)
    gain = sg_ref[...] * (1.0 - lambda_init)
    last = qi % 2
    for hh in range(heads):
        u1, u2 = 2 * hh, 2 * hh + 1
        ot = (acc_ref[last, u1] / l_ref[last, u1]
              - lam * (acc_ref[last, u2] / l_ref[last, u2]))
        ms = jnp.mean(ot * ot, axis=0, keepdims=True)
        ot = ot * lax.rsqrt(ms + SUBLN_EPS)
        o_ref[:, hh * V_HEAD_DIM:(hh + 1) * V_HEAD_DIM] = (ot.T * gain).astype(BF16)


def _diff_attention(qvt, k, lq1, lk1, lq2, lk2, subln_g, *, batch, seq, lambda_init,
                    tq=512, heads=2):
    t = k.shape[0]
    tk = tq
    q_tiles = seq // tq
    head_groups = N_DIFF_HEADS // heads
    width = heads * V_HEAD_DIM
    units = 2 * heads
    vec = lambda a: a.reshape(1, -1)
    small = lambda width: pl.BlockSpec((1, width), lambda b, h, i: (0, 0))
    return pl.pallas_call(
        functools.partial(_attn_kernel, tq=tq, tk=tk, heads=heads, lambda_init=lambda_init),
        grid=(batch, head_groups, q_tiles),
        in_specs=[
            pl.BlockSpec((width, tq), lambda b, h, i: (h, b * q_tiles + i)),
            pl.BlockSpec((seq, width), lambda b, h, i: (b, h)),
            pl.BlockSpec((width, seq), lambda b, h, i: (head_groups + h, b)),
            small(HEAD_DIM), small(HEAD_DIM), small(HEAD_DIM), small(HEAD_DIM),
            small(V_HEAD_DIM),
        ],
        out_specs=pl.BlockSpec((tq, width), lambda b, h, i: (b * q_tiles + i, h)),
        out_shape=jax.ShapeDtypeStruct((t, N_DIFF_HEADS * V_HEAD_DIM), BF16),
        scratch_shapes=[
            pltpu.VMEM((units, tk, tq), F32), pltpu.VMEM((units, tk, tq), F32),
            pltpu.VMEM((units, 1, tq), F32), pltpu.VMEM((units, 1, tq), F32),
            pltpu.VMEM((units, tk, tq), BF16), pltpu.VMEM((units, tk, tq), BF16),
            pltpu.VMEM((units, 1, tq), F32), pltpu.VMEM((units, 1, tq), F32),
            pltpu.VMEM((2, units, 1, tq), F32),
            pltpu.VMEM((2, units, 1, tq), F32),
            pltpu.VMEM((2, units, V_HEAD_DIM, tq), F32),
        ],
        compiler_params=_params(("parallel", "parallel", "arbitrary")),
        name="diff_attention",
    )(qvt, k, qvt, vec(lq1), vec(lk1), vec(lq2), vec(lk2), vec(subln_g))


def _emit_residual(h, g_ref, out_refs, final):
    if final:
        out_refs[0][...] = _rms_norm_rows(h, g_ref[...], RMS_EPS)
    else:
        out_refs[0][...] = h
        out_refs[1][...] = _rms_norm_rows(h, g_ref[...], RMS_EPS).astype(BF16)


def _residual_out(t, d, tm, final):
    spec = pl.BlockSpec((tm, d), lambda i: (i, 0))
    if final:
        return [spec], [jax.ShapeDtypeStruct((t, d), F32)]
    return [spec, spec], [jax.ShapeDtypeStruct((t, d), F32), jax.ShapeDtypeStruct((t, d), BF16)]


def _proj_residual_kernel(a_ref, w_ref, r_ref, g_ref, *out_refs, final):
    h = r_ref[...] + jnp.dot(a_ref[...], w_ref[...], preferred_element_type=F32)
    _emit_residual(h, g_ref, out_refs, final)


def _proj_residual(a, w, layer, res, g_next, *, final=False, tm=512):
    t, k = a.shape
    n = w.shape[2]
    out_specs, out_shape = _residual_out(t, n, tm, final)
    return pl.pallas_call(
        functools.partial(_proj_residual_kernel, final=final),
        grid=(t // tm,),
        in_specs=[
            pl.BlockSpec((tm, k), lambda i: (i, 0)),
            pl.BlockSpec((None, k, n), lambda i: (layer, 0, 0), pipeline_mode=pl.Buffered(1)),
            pl.BlockSpec((tm, n), lambda i: (i, 0)),
            pl.BlockSpec((1, n), lambda i: (0, 0)),
        ],
        out_specs=out_specs,
        out_shape=out_shape,
        compiler_params=_params(("parallel",)),
        name="proj_residual",
    )(a, w, res, g_next)


def _bch_kernel(xn_ref, wb_ref, wc_ref, wu_ref, gb_ref, z_ref):
    xn = xn_ref[...]
    gb_ref[...] = jnp.dot(xn, wb_ref[...], preferred_element_type=F32).astype(BF16)
    gate_c = jnp.dot(xn, wc_ref[...], preferred_element_type=F32)
    u = jnp.dot(xn, wu_ref[...], preferred_element_type=F32)
    z_ref[...] = (gate_c * u).astype(BF16)


def _bch_proj(xn, w, layer, *, tm=1024, tn=512):
    t, d = xn.shape
    col_tiles = d // tn
    w_spec = lambda part: pl.BlockSpec((None, d, tn),
                                       lambda i, j: (layer, 0, part * col_tiles + j))
    out_spec = pl.BlockSpec((tm, tn), lambda i, j: (i, j))
    return pl.pallas_call(
        _bch_kernel,
        grid=(t // tm, col_tiles),
        in_specs=[pl.BlockSpec((tm, d), lambda i, j: (i, 0)), w_spec(0), w_spec(1), w_spec(2)],
        out_specs=[out_spec, out_spec],
        out_shape=[jax.ShapeDtypeStruct((t, d), BF16)] * 2,
        compiler_params=_params(("parallel", "parallel")),
        name="bch_proj",
    )(xn, w, w, w)


def _conv_out_kernel(gb_ref, z_ref, halo_ref, cw_ref, w_ref, r_ref, g_ref, *out_refs,
                     seq_tiles, final):
    tm = z_ref.shape[0]
    z = z_ref[...].astype(F32)
    halo = jnp.where(pl.program_id(0) % seq_tiles == 0, 0.0, halo_ref[...].astype(F32))
    zext = jnp.concatenate([halo, z], axis=0)
    pad = halo.shape[0]
    cw = cw_ref[...]
    zc = cw[2:3, :] * z
    zc = zc + cw[1:2, :] * zext[pad - 1:pad - 1 + tm, :]
    zc = zc + cw[0:1, :] * zext[pad - 2:pad - 2 + tm, :]
    a = (gb_ref[...].astype(F32) * zc).astype(BF16)
    h = r_ref[...] + jnp.dot(a, w_ref[...], preferred_element_type=F32)
    _emit_residual(h, g_ref, out_refs, final)


def _conv_out_proj(gate_b, z, conv_w, w, layer, res, g_next, *, seq, final=False, tm=512):
    t, d = z.shape
    n = w.shape[2]
    halo_rows = 2 * V7X_F32_SUBLANES
    halo_per_tile = tm // halo_rows
    row_spec = pl.BlockSpec((tm, d), lambda i: (i, 0))
    out_specs, out_shape = _residual_out(t, n, tm, final)
    return pl.pallas_call(
        functools.partial(_conv_out_kernel, seq_tiles=seq // tm, final=final),
        grid=(t // tm,),
        in_specs=[
            row_spec,
            row_spec,
            pl.BlockSpec((halo_rows, d), lambda i: (jnp.maximum(i * halo_per_tile - 1, 0), 0)),
            pl.BlockSpec((None, CONV_WIDTH, d), lambda i: (layer, 0, 0)),
            pl.BlockSpec((None, d, n), lambda i: (layer, 0, 0), pipeline_mode=pl.Buffered(1)),
            pl.BlockSpec((tm, n), lambda i: (i, 0)),
            pl.BlockSpec((1, n), lambda i: (0, 0)),
        ],
        out_specs=out_specs,
        out_shape=out_shape,
        compiler_params=_params(("parallel",)),
        name="conv_out_proj",
    )(gate_b, z, z, conv_w, w, res, g_next)


def _ffn_in_kernel(xn_ref, wg_ref, wu_ref, o_ref):
    xn = xn_ref[...]
    gate = jnp.dot(xn, wg_ref[...], preferred_element_type=F32)
    up = jnp.dot(xn, wu_ref[...], preferred_element_type=F32)
    o_ref[...] = (gate * jax.nn.sigmoid(gate) * up).astype(BF16)


def _ffn_in(xn, w_gate, w_up, layer, *, tm=1024, tf=512):
    t, d = xn.shape
    f = w_gate.shape[2]
    w_spec = pl.BlockSpec((None, d, tf), lambda i, j: (layer, 0, j))
    return pl.pallas_call(
        _ffn_in_kernel,
        grid=(t // tm, f // tf),
        in_specs=[pl.BlockSpec((tm, d), lambda i, j: (i, 0)), w_spec, w_spec],
        out_specs=pl.BlockSpec((tm, tf), lambda i, j: (i, j)),
        out_shape=jax.ShapeDtypeStruct((t, f), BF16),
        compiler_params=_params(("parallel", "parallel")),
        name="ffn_in",
    )(xn, w_gate, w_up)


def _first_norm_kernel(x_ref, g_ref, o_ref):
    o_ref[...] = _rms_norm_rows(x_ref[...], g_ref[...], RMS_EPS).astype(BF16)


def _first_norm(h, g, *, tm=1024):
    t, d = h.shape
    return pl.pallas_call(
        _first_norm_kernel,
        grid=(t // tm,),
        in_specs=[pl.BlockSpec((tm, d), lambda i: (i, 0)),
                  pl.BlockSpec((1, d), lambda i: (0, 0))],
        out_specs=pl.BlockSpec((tm, d), lambda i: (i, 0)),
        out_shape=jax.ShapeDtypeStruct((t, d), BF16),
        compiler_params=_params(("parallel",)),
        name="first_norm",
    )(h, g)


def _rope_tables(seq):
    pos = jnp.arange(seq, dtype=F32)
    inv_freq = 1.0 / (ROPE_THETA ** (jnp.arange(0, HEAD_DIM, 2, dtype=F32) / HEAD_DIM))
    ang = pos[:, None] * inv_freq[None, :]
    cos, sin = jnp.cos(ang), jnp.sin(ang)
    return (jnp.concatenate([cos, cos], axis=-1),
            jnp.concatenate([-sin, sin], axis=-1))


def kernel(x, attn_norm_g, w_qkv, w_o_attn, lambda_q1, lambda_k1, lambda_q2, lambda_k2,
           subln_g, conv_norm_g, w_bch, conv_w, w_o_conv, ffn_norm_g, w_gate, w_up,
           w_down, final_norm_g):
    batch, seq, d = x.shape
    rope_a, rope_b = _rope_tables(seq)
    rope_a_t, rope_b_t = rope_a.T, rope_b.T
    row = lambda a: a.reshape(1, -1)
    w_qkv_b, w_o_attn_b, w_bch_b, w_o_conv_b = (
        a.astype(BF16) for a in (w_qkv, w_o_attn, w_bch, w_o_conv))
    w_gate_b, w_up_b, w_down_b = (a.astype(BF16) for a in (w_gate, w_up, w_down))
    w_qkv_b = lax.optimization_barrier(w_qkv_b)
    w_qv_t = jnp.swapaxes(
        jnp.concatenate([w_qkv_b[:, :, :QK_WIDTH], w_qkv_b[:, :, 2 * QK_WIDTH:]], axis=2), 1, 2)

    def mixer_norm_g(layer):
        j = layer // 2
        return row(attn_norm_g[j] if layer % 2 == 0 else conv_norm_g[j])

    h = x.reshape(batch * seq, d)
    xn = _first_norm(h, mixer_norm_g(0))
    for i in range(DEPTH):
        j = i // 2
        g_ffn = row(ffn_norm_g[i])
        if i % 2 == 0:
            lambda_init = 0.8 - 0.6 * math.exp(-0.3 * i)
            k = _k_proj(xn, w_qkv_b, j, rope_a, rope_b, seq=seq)
            qvt = _qvt_proj(xn, w_qv_t, j, rope_a_t, rope_b_t, seq=seq)
            o = _diff_attention(qvt, k, lambda_q1[j], lambda_k1[j], lambda_q2[j],
                                lambda_k2[j], subln_g[j], batch=batch, seq=seq,
                                lambda_init=lambda_init)
            h, xn = _proj_residual(o, w_o_attn_b, j, h, g_ffn)
        else:
            gate_b, z = _bch_proj(xn, w_bch_b, j)
            h, xn = _conv_out_proj(gate_b, z, conv_w, w_o_conv_b, j, h, g_ffn, seq=seq)
        hid = _ffn_in(xn, w_gate_b, w_up_b, i)
        if i + 1 < DEPTH:
            h, xn = _proj_residual(hid, w_down_b, i, h, mixer_norm_g(i + 1))
        else:
            (out,) = _proj_residual(hid, w_down_b, i, h, row(final_norm_g), final=True)
    return out.reshape(batch, seq, d)
```

```python
import functools
import math

import jax
import jax.numpy as jnp
from jax import lax
from jax.experimental import pallas as pl
from jax.experimental.pallas import tpu as pltpu

D_MODEL = 2048
DEPTH = 4
N_DIFF_HEADS = 8
HEAD_DIM = 128
V_HEAD_DIM = 2 * HEAD_DIM
QK_WIDTH = 2 * N_DIFF_HEADS * HEAD_DIM
ROPE_THETA = 10000.0
CONV_WIDTH = 3
RMS_EPS = 1e-6
SUBLN_EPS = 1e-5

V7X_LANES = 128
V7X_F32_SUBLANES = 8
V7X_VMEM_BYTES = 64 * 1024 * 1024
VMEM_LIMIT_BYTES = V7X_VMEM_BYTES - 6 * 1024 * 1024

MASK_VALUE = -1e30
LOG2_E = math.log2(math.e)

F32 = jnp.float32
BF16 = jnp.bfloat16


def _params(semantics):
    return pltpu.CompilerParams(dimension_semantics=semantics,
                                vmem_limit_bytes=VMEM_LIMIT_BYTES)


def _run_if(cond, fn):
    def body(_, carry):
        fn()
        return carry
    lax.fori_loop(0, cond.astype(jnp.int32), body, 0)


def _cast_weights_once(pairs):
    @pl.when(pl.program_id(1) == 0)
    def _():
        for src_ref, dst_ref in pairs:
            dst_ref[...] = src_ref[...].astype(BF16)


def _rms_norm_rows(x, g, eps):
    ms = jnp.mean(x * x, axis=-1, keepdims=True)
    return x * lax.rsqrt(ms + eps) * g


def _k_kernel(xn_ref, w_ref, ra_ref, rb_ref, o_ref, w_bf_ref, *, tn):
    _cast_weights_once([(w_ref, w_bf_ref)])
    acc = jnp.dot(xn_ref[...], w_bf_ref[...], preferred_element_type=F32)
    ra = ra_ref[...]
    rb = rb_ref[...]
    for c in range(tn // HEAD_DIM):
        t = acc[:, c * HEAD_DIM:(c + 1) * HEAD_DIM]
        r = t * ra + pltpu.roll(t, HEAD_DIM // 2, axis=1) * rb
        o_ref[:, c * HEAD_DIM:(c + 1) * HEAD_DIM] = r.astype(BF16)


def _k_proj(xn, w, layer, ra, rb, *, seq, tm=1024, tn=1024):
    t, d = xn.shape
    seq_tiles = seq // tm
    k_tiles = QK_WIDTH // tn
    return pl.pallas_call(
        functools.partial(_k_kernel, tn=tn),
        grid=(k_tiles, t // tm),
        in_specs=[
            pl.BlockSpec((tm, d), lambda j, i: (i, 0)),
            pl.BlockSpec((None, d, tn), lambda j, i: (layer, 0, k_tiles + j)),
            pl.BlockSpec((tm, HEAD_DIM), lambda j, i: (i % seq_tiles, 0)),
            pl.BlockSpec((tm, HEAD_DIM), lambda j, i: (i % seq_tiles, 0)),
        ],
        out_specs=pl.BlockSpec((tm, tn), lambda j, i: (i, j)),
        out_shape=jax.ShapeDtypeStruct((t, QK_WIDTH), BF16),
        scratch_shapes=[pltpu.VMEM((d, tn), BF16)],
        compiler_params=_params(("parallel", "arbitrary")),
        name="k_proj",
    )(xn, w, ra, rb)


def _qvt_kernel(xn_ref, w_ref, rat_ref, rbt_ref, o_ref, wt_bf_ref, *, tn):
    j = pl.program_id(0)
    n_q_tiles = QK_WIDTH // tn

    @pl.when(pl.program_id(1) == 0)
    def _():
        wt_bf_ref[...] = w_ref[...].T.astype(BF16)

    acc = lax.dot_general(wt_bf_ref[...], xn_ref[...], (((1,), (1,)), ((), ())),
                          preferred_element_type=F32)

    @pl.when(j < n_q_tiles)
    def _():
        scale = LOG2_E * HEAD_DIM ** -0.5
        rat = rat_ref[...] * scale
        rbt = rbt_ref[...] * scale
        half = HEAD_DIM // 2
        for c in range(tn // HEAD_DIM):
            t = acc[c * HEAD_DIM:(c + 1) * HEAD_DIM, :]
            swapped = jnp.concatenate([t[half:, :], t[:half, :]], axis=0)
            o_ref[c * HEAD_DIM:(c + 1) * HEAD_DIM, :] = (t * rat + swapped * rbt).astype(BF16)

    @pl.when(j >= n_q_tiles)
    def _():
        o_ref[...] = acc.astype(BF16)


def _qvt_proj(xn, w, layer, rat, rbt, *, seq, tm=1024, tn=1024):
    t, d = xn.shape
    n = 2 * QK_WIDTH
    seq_tiles = seq // tm
    n_q_tiles = QK_WIDTH // tn
    w_col = lambda j: jnp.where(j < n_q_tiles, j, j + n_q_tiles)
    return pl.pallas_call(
        functools.partial(_qvt_kernel, tn=tn),
        grid=(n // tn, t // tm),
        in_specs=[
            pl.BlockSpec((tm, d), lambda j, i: (i, 0)),
            pl.BlockSpec((None, d, tn), lambda j, i: (layer, 0, w_col(j))),
            pl.BlockSpec((HEAD_DIM, tm), lambda j, i: (0, i % seq_tiles)),
            pl.BlockSpec((HEAD_DIM, tm), lambda j, i: (0, i % seq_tiles)),
        ],
        out_specs=pl.BlockSpec((tn, tm), lambda j, i: (j, i)),
        out_shape=jax.ShapeDtypeStruct((n, t), BF16),
        scratch_shapes=[pltpu.VMEM((tn, d), BF16)],
        compiler_params=_params(("parallel", "arbitrary")),
        name="qvt_proj",
    )(xn, w, rat, rbt)


def _attn_kernel(qt_ref, k_ref, vt_ref, lq1_ref, lk1_ref, lq2_ref, lk2_ref, sg_ref,
                 o_ref, s0_ref, s1_ref, c0_ref, c1_ref, p0_ref, p1_ref, a0_ref, a1_ref,
                 m_ref, l_ref, acc_ref, *, tq, tk, heads, lambda_init):
    qi = pl.program_id(2)
    units = [(hh, c) for hh in range(heads) for c in range(2)]
    s_refs = (s0_ref, s1_ref)
    p_refs, alpha_refs = (p0_ref, p1_ref), (a0_ref, a1_ref)
    chunk_max_refs = (c0_ref, c1_ref)

    m_ref[1] = jnp.full(m_ref.shape[1:], MASK_VALUE, F32)
    l_ref[1] = jnp.zeros(l_ref.shape[1:], F32)
    acc_ref[1] = jnp.zeros(acc_ref.shape[1:], F32)

    def kv_start(kj):
        return pl.multiple_of(kj * tk, tk)

    def scores(kj, slot, only=None):
        for u, (hh, c) in enumerate(units):
            if only is not None and hh != only:
                continue
            start = hh * V_HEAD_DIM + c * HEAD_DIM
            part = slice(start, start + HEAD_DIM)
            s = jnp.dot(k_ref[pl.ds(kv_start(kj), tk), part], qt_ref[part, :],
                        preferred_element_type=F32)
            s_refs[slot][u] = s
            chunk_max_refs[slot][u] = jnp.max(s, axis=0, keepdims=True)

    def softmax(slot, masked, only=None):
        prev = 1 - slot
        for c, (hh, _) in enumerate(units):
            if only is not None and hh != only:
                continue
            s = s_refs[slot][c]
            if masked:
                key = lax.broadcasted_iota(jnp.int32, (tk, tq), 0)
                query = lax.broadcasted_iota(jnp.int32, (tk, tq), 1)
                s = jnp.where(key <= query, s, MASK_VALUE)
                chunk_max = jnp.max(s, axis=0, keepdims=True)
            else:
                chunk_max = chunk_max_refs[slot][c]
            m_prev = m_ref[prev, c]
            m_new = jnp.maximum(m_prev, chunk_max)
            alpha = jnp.exp2(m_prev - m_new)
            p = jnp.exp2(s - m_new)
            l_ref[slot, c] = alpha * l_ref[prev, c] + jnp.sum(p, axis=0, keepdims=True)
            m_ref[slot, c] = m_new
            alpha_refs[slot][c] = alpha
            p_refs[slot][c] = p.astype(BF16)

    def accumulate(kj, slot, only=None):
        for u, (hh, _) in enumerate(units):
            if only is not None and hh != only:
                continue
            vt = vt_ref[hh * V_HEAD_DIM:(hh + 1) * V_HEAD_DIM, pl.ds(kv_start(kj), tk)]
            pv = jnp.dot(vt, p_refs[slot][u], preferred_element_type=F32)
            acc_ref[slot, u] = alpha_refs[slot][u] * acc_ref[1 - slot, u] + pv

    def pipeline_step(kj, slot, only=None):
        scores(kj + 2, slot, only)
        softmax(1 - slot, masked=False, only=only)
        accumulate(kj, slot, only)

    scores(0, 0)

    def fill():
        scores(1, 1)
        softmax(0, masked=False)

    _run_if(qi >= 1, fill)

    steady_steps = qi - 1

    def pair(u, carry):
        for hh in range(heads):
            pipeline_step(2 * u, 0, hh)
            pipeline_step(2 * u + 1, 1, hh)
        return carry

    lax.fori_loop(0, steady_steps // 2, pair, 0)

    qi_even = qi % 2 == 0
    qi_odd = jnp.logical_not(qi_even)
    even_with_prev = jnp.logical_and(qi >= 2, qi_even)
    _run_if(even_with_prev, lambda: [pipeline_step(qi - 2, 0, hh) for hh in range(heads)])

    def drain(slot, with_prev=True):
        for hh in range(heads):
            softmax(slot, masked=True, only=hh)
            if with_prev:
                accumulate(qi - 1, 1 - slot, hh)
            accumulate(qi, slot, hh)

    _run_if(even_with_prev, lambda: drain(0))
    _run_if(qi_odd, lambda: drain(1))
    _run_if(qi == 0, lambda: drain(0, with_prev=False))

    lam = (jnp.exp(jnp.sum(lq1_ref[...] * lk1_ref[...], axis=-1, keepdims=True))
           - jnp.exp(jnp.sum(lq2_ref[...] * lk2_ref[...], axis=-1, keepdims=True))
           + lambda_init)
    gain = sg_ref[...] * (1.0 - lambda_init)
    last = qi % 2
    for hh in range(heads):
        u1, u2 = 2 * hh, 2 * hh + 1
        ot = (acc_ref[last, u1] / l_ref[last, u1]
              - lam * (acc_ref[last, u2] / l_ref[last, u2]))
        ms = jnp.mean(ot * ot, axis=0, keepdims=True)
        ot = ot * lax.rsqrt(ms + SUBLN_EPS)
        o_ref[:, hh * V_HEAD_DIM:(hh + 1) * V_HEAD_DIM] = (ot.T * gain).astype(BF16)


def _diff_attention(qvt, k, lq1, lk1, lq2, lk2, subln_g, *, batch, seq, lambda_init,
                    tq=512, heads=2):
    t = k.shape[0]
    tk = tq
    q_tiles = seq // tq
    head_groups = N_DIFF_HEADS // heads
    width = heads * V_HEAD_DIM
    units = 2 * heads
    vec = lambda a: a.reshape(1, -1)
    small = lambda width: pl.BlockSpec((1, width), lambda b, h, i: (0, 0))
    return pl.pallas_call(
        functools.partial(_attn_kernel, tq=tq, tk=tk, heads=heads, lambda_init=lambda_init),
        grid=(batch, head_groups, q_tiles),
        in_specs=[
            pl.BlockSpec((width, tq), lambda b, h, i: (h, b * q_tiles + i)),
            pl.BlockSpec((seq, width), lambda b, h, i: (b, h)),
            pl.BlockSpec((width, seq), lambda b, h, i: (head_groups + h, b)),
            small(HEAD_DIM), small(HEAD_DIM), small(HEAD_DIM), small(HEAD_DIM),
            small(V_HEAD_DIM),
        ],
        out_specs=pl.BlockSpec((tq, width), lambda b, h, i: (b * q_tiles + i, h)),
        out_shape=jax.ShapeDtypeStruct((t, N_DIFF_HEADS * V_HEAD_DIM), BF16),
        scratch_shapes=[
            pltpu.VMEM((units, tk, tq), F32), pltpu.VMEM((units, tk, tq), F32),
            pltpu.VMEM((units, 1, tq), F32), pltpu.VMEM((units, 1, tq), F32),
            pltpu.VMEM((units, tk, tq), BF16), pltpu.VMEM((units, tk, tq), BF16),
            pltpu.VMEM((units, 1, tq), F32), pltpu.VMEM((units, 1, tq), F32),
            pltpu.VMEM((2, units, 1, tq), F32),
            pltpu.VMEM((2, units, 1, tq), F32),
            pltpu.VMEM((2, units, V_HEAD_DIM, tq), F32),
        ],
        compiler_params=_params(("parallel", "parallel", "arbitrary")),
        name="diff_attention",
    )(qvt, k, qvt, vec(lq1), vec(lk1), vec(lq2), vec(lk2), vec(subln_g))


def _emit_residual(h, g_ref, out_refs, final):
    if final:
        out_refs[0][...] = _rms_norm_rows(h, g_ref[...], RMS_EPS)
    else:
        out_refs[0][...] = h
        out_refs[1][...] = _rms_norm_rows(h, g_ref[...], RMS_EPS).astype(BF16)


def _residual_out(t, d, tm, final):
    spec = pl.BlockSpec((tm, d), lambda i: (i, 0))
    if final:
        return [spec], [jax.ShapeDtypeStruct((t, d), F32)]
    return [spec, spec], [jax.ShapeDtypeStruct((t, d), F32), jax.ShapeDtypeStruct((t, d), BF16)]


def _proj_residual_kernel(a_ref, w_ref, r_ref, g_ref, *out_refs, final):
    h = r_ref[...] + jnp.dot(a_ref[...], w_ref[...], preferred_element_type=F32)
    _emit_residual(h, g_ref, out_refs, final)


def _proj_residual(a, w, layer, res, g_next, *, final=False, tm=512):
    t, k = a.shape
    n = w.shape[2]
    out_specs, out_shape = _residual_out(t, n, tm, final)
    return pl.pallas_call(
        functools.partial(_proj_residual_kernel, final=final),
        grid=(t // tm,),
        in_specs=[
            pl.BlockSpec((tm, k), lambda i: (i, 0)),
            pl.BlockSpec((None, k, n), lambda i: (layer, 0, 0), pipeline_mode=pl.Buffered(1)),
            pl.BlockSpec((tm, n), lambda i: (i, 0)),
            pl.BlockSpec((1, n), lambda i: (0, 0)),
        ],
        out_specs=out_specs,
        out_shape=out_shape,
        compiler_params=_params(("parallel",)),
        name="proj_residual",
    )(a, w, res, g_next)


def _bch_kernel(xn_ref, wb_ref, wc_ref, wu_ref, gb_ref, z_ref, wb_bf_ref, wc_bf_ref,
                wu_bf_ref):
    _cast_weights_once([(wb_ref, wb_bf_ref), (wc_ref, wc_bf_ref), (wu_ref, wu_bf_ref)])
    xn = xn_ref[...]
    gb_ref[...] = jnp.dot(xn, wb_bf_ref[...], preferred_element_type=F32).astype(BF16)
    gate_c = jnp.dot(xn, wc_bf_ref[...], preferred_element_type=F32)
    u = jnp.dot(xn, wu_bf_ref[...], preferred_element_type=F32)
    z_ref[...] = (gate_c * u).astype(BF16)


def _bch_proj(xn, w, layer, *, tm=1024, tn=512):
    t, d = xn.shape
    col_tiles = d // tn
    w_spec = lambda part: pl.BlockSpec((None, d, tn),
                                       lambda j, i: (layer, 0, part * col_tiles + j))
    out_spec = pl.BlockSpec((tm, tn), lambda j, i: (i, j))
    return pl.pallas_call(
        _bch_kernel,
        grid=(col_tiles, t // tm),
        in_specs=[pl.BlockSpec((tm, d), lambda j, i: (i, 0)), w_spec(0), w_spec(1), w_spec(2)],
        out_specs=[out_spec, out_spec],
        out_shape=[jax.ShapeDtypeStruct((t, d), BF16)] * 2,
        scratch_shapes=[pltpu.VMEM((d, tn), BF16)] * 3,
        compiler_params=_params(("parallel", "arbitrary")),
        name="bch_proj",
    )(xn, w, w, w)


def _conv_out_kernel(gb_ref, z_ref, halo_ref, cw_ref, w_ref, r_ref, g_ref, *out_refs,
                     seq_tiles, final):
    tm = z_ref.shape[0]
    z = z_ref[...].astype(F32)
    halo = jnp.where(pl.program_id(0) % seq_tiles == 0, 0.0, halo_ref[...].astype(F32))
    zext = jnp.concatenate([halo, z], axis=0)
    pad = halo.shape[0]
    cw = cw_ref[...]
    zc = cw[2:3, :] * z
    zc = zc + cw[1:2, :] * zext[pad - 1:pad - 1 + tm, :]
    zc = zc + cw[0:1, :] * zext[pad - 2:pad - 2 + tm, :]
    a = (gb_ref[...].astype(F32) * zc).astype(BF16)
    h = r_ref[...] + jnp.dot(a, w_ref[...], preferred_element_type=F32)
    _emit_residual(h, g_ref, out_refs, final)


def _conv_out_proj(gate_b, z, conv_w, w, layer, res, g_next, *, seq, final=False, tm=512):
    t, d = z.shape
    n = w.shape[2]
    halo_rows = 2 * V7X_F32_SUBLANES
    halo_per_tile = tm // halo_rows
    row_spec = pl.BlockSpec((tm, d), lambda i: (i, 0))
    out_specs, out_shape = _residual_out(t, n, tm, final)
    return pl.pallas_call(
        functools.partial(_conv_out_kernel, seq_tiles=seq // tm, final=final),
        grid=(t // tm,),
        in_specs=[
            row_spec,
            row_spec,
            pl.BlockSpec((halo_rows, d), lambda i: (jnp.maximum(i * halo_per_tile - 1, 0), 0)),
            pl.BlockSpec((None, CONV_WIDTH, d), lambda i: (layer, 0, 0)),
            pl.BlockSpec((None, d, n), lambda i: (layer, 0, 0), pipeline_mode=pl.Buffered(1)),
            pl.BlockSpec((tm, n), lambda i: (i, 0)),
            pl.BlockSpec((1, n), lambda i: (0, 0)),
        ],
        out_specs=out_specs,
        out_shape=out_shape,
        compiler_params=_params(("parallel",)),
        name="conv_out_proj",
    )(gate_b, z, z, conv_w, w, res, g_next)


def _ffn_in_kernel(xn_ref, wg_ref, wu_ref, o_ref, wg_bf_ref, wu_bf_ref):
    _cast_weights_once([(wg_ref, wg_bf_ref), (wu_ref, wu_bf_ref)])
    xn = xn_ref[...]
    gate = jnp.dot(xn, wg_bf_ref[...], preferred_element_type=F32)
    up = jnp.dot(xn, wu_bf_ref[...], preferred_element_type=F32)
    o_ref[...] = (gate * jax.nn.sigmoid(gate) * up).astype(BF16)


def _ffn_in(xn, w_gate, w_up, layer, *, tm=1024, tf=512):
    t, d = xn.shape
    f = w_gate.shape[2]
    w_spec = pl.BlockSpec((None, d, tf), lambda j, i: (layer, 0, j))
    return pl.pallas_call(
        _ffn_in_kernel,
        grid=(f // tf, t // tm),
        in_specs=[pl.BlockSpec((tm, d), lambda j, i: (i, 0)), w_spec, w_spec],
        out_specs=pl.BlockSpec((tm, tf), lambda j, i: (i, j)),
        out_shape=jax.ShapeDtypeStruct((t, f), BF16),
        scratch_shapes=[pltpu.VMEM((d, tf), BF16)] * 2,
        compiler_params=_params(("parallel", "arbitrary")),
        name="ffn_in",
    )(xn, w_gate, w_up)


def _first_norm_kernel(x_ref, g_ref, o_ref):
    o_ref[...] = _rms_norm_rows(x_ref[...], g_ref[...], RMS_EPS).astype(BF16)


def _first_norm(h, g, *, tm=1024):
    t, d = h.shape
    return pl.pallas_call(
        _first_norm_kernel,
        grid=(t // tm,),
        in_specs=[pl.BlockSpec((tm, d), lambda i: (i, 0)),
                  pl.BlockSpec((1, d), lambda i: (0, 0))],
        out_specs=pl.BlockSpec((tm, d), lambda i: (i, 0)),
        out_shape=jax.ShapeDtypeStruct((t, d), BF16),
        compiler_params=_params(("parallel",)),
        name="first_norm",
    )(h, g)


def _rope_tables(seq):
    pos = jnp.arange(seq, dtype=F32)
    inv_freq = 1.0 / (ROPE_THETA ** (jnp.arange(0, HEAD_DIM, 2, dtype=F32) / HEAD_DIM))
    ang = pos[:, None] * inv_freq[None, :]
    cos, sin = jnp.cos(ang), jnp.sin(ang)
    return (jnp.concatenate([cos, cos], axis=-1),
            jnp.concatenate([-sin, sin], axis=-1))


def kernel(x, attn_norm_g, w_qkv, w_o_attn, lambda_q1, lambda_k1, lambda_q2, lambda_k2,
           subln_g, conv_norm_g, w_bch, conv_w, w_o_conv, ffn_norm_g, w_gate, w_up,
           w_down, final_norm_g):
    batch, seq, d = x.shape
    rope_a, rope_b = _rope_tables(seq)
    rope_a_t, rope_b_t = rope_a.T, rope_b.T
    row = lambda a: a.reshape(1, -1)
    w_o_attn_b, w_o_conv_b, w_down_b = (a.astype(BF16) for a in (w_o_attn, w_o_conv, w_down))

    def mixer_norm_g(layer):
        j = layer // 2
        return row(attn_norm_g[j] if layer % 2 == 0 else conv_norm_g[j])

    h = x.reshape(batch * seq, d)
    xn = _first_norm(h, mixer_norm_g(0))
    for i in range(DEPTH):
        j = i // 2
        g_ffn = row(ffn_norm_g[i])
        if i % 2 == 0:
            lambda_init = 0.8 - 0.6 * math.exp(-0.3 * i)
            k = _k_proj(xn, w_qkv, j, rope_a, rope_b, seq=seq)
            qvt = _qvt_proj(xn, w_qkv, j, rope_a_t, rope_b_t, seq=seq)
            o = _diff_attention(qvt, k, lambda_q1[j], lambda_k1[j], lambda_q2[j],
                                lambda_k2[j], subln_g[j], batch=batch, seq=seq,
                                lambda_init=lambda_init)
            h, xn = _proj_residual(o, w_o_attn_b, j, h, g_ffn)
        else:
            gate_b, z = _bch_proj(xn, w_bch, j)
            h, xn = _conv_out_proj(gate_b, z, conv_w, w_o_conv_b, j, h, g_ffn, seq=seq)
        hid = _ffn_in(xn, w_gate, w_up, i)
        if i + 1 < DEPTH:
            h, xn = _proj_residual(hid, w_down_b, i, h, mixer_norm_g(i + 1))
        else:
            (out,) = _proj_residual(hid, w_down_b, i, h, row(final_norm_g), final=True)
    return out.reshape(batch, seq, d)
```

```python
import functools
import math

import jax
import jax.numpy as jnp
from jax import lax
from jax.experimental import pallas as pl
from jax.experimental.pallas import tpu as pltpu

D_MODEL = 2048
DEPTH = 4
N_DIFF_HEADS = 8
HEAD_DIM = 128
V_HEAD_DIM = 2 * HEAD_DIM
QK_WIDTH = 2 * N_DIFF_HEADS * HEAD_DIM
ROPE_THETA = 10000.0
CONV_WIDTH = 3
RMS_EPS = 1e-6
SUBLN_EPS = 1e-5

V7X_LANES = 128
V7X_F32_SUBLANES = 8
V7X_VMEM_BYTES = 64 * 1024 * 1024
VMEM_LIMIT_BYTES = V7X_VMEM_BYTES - 6 * 1024 * 1024

MASK_VALUE = -1e30
LOG2_E = math.log2(math.e)

F32 = jnp.float32
BF16 = jnp.bfloat16


def _params(semantics):
    return pltpu.CompilerParams(dimension_semantics=semantics,
                                vmem_limit_bytes=VMEM_LIMIT_BYTES)


def _run_if(cond, fn):
    def body(_, carry):
        fn()
        return carry
    lax.fori_loop(0, cond.astype(jnp.int32), body, 0)


def _cast_weights_once(pairs):
    @pl.when(pl.program_id(1) == 0)
    def _():
        for src_ref, dst_ref in pairs:
            dst_ref[...] = src_ref[...].astype(BF16)


def _rms_norm_rows(x, g, eps):
    ms = jnp.mean(x * x, axis=-1, keepdims=True)
    return x * lax.rsqrt(ms + eps) * g


def _k_kernel(xn_ref, w_ref, ra_ref, rb_ref, o_ref, w_bf_ref, *, tn):
    _cast_weights_once([(w_ref, w_bf_ref)])
    acc = jnp.dot(xn_ref[...], w_bf_ref[...], preferred_element_type=F32)
    ra = ra_ref[...]
    rb = rb_ref[...]
    for c in range(tn // HEAD_DIM):
        t = acc[:, c * HEAD_DIM:(c + 1) * HEAD_DIM]
        r = t * ra + pltpu.roll(t, HEAD_DIM // 2, axis=1) * rb
        o_ref[:, c * HEAD_DIM:(c + 1) * HEAD_DIM] = r.astype(BF16)


def _k_proj(xn, w, layer, ra, rb, *, seq, tm=1024, tn=1024):
    t, d = xn.shape
    seq_tiles = seq // tm
    k_tiles = QK_WIDTH // tn
    return pl.pallas_call(
        functools.partial(_k_kernel, tn=tn),
        grid=(k_tiles, t // tm),
        in_specs=[
            pl.BlockSpec((tm, d), lambda j, i: (i, 0)),
            pl.BlockSpec((None, d, tn), lambda j, i: (layer, 0, k_tiles + j)),
            pl.BlockSpec((tm, HEAD_DIM), lambda j, i: (i % seq_tiles, 0)),
            pl.BlockSpec((tm, HEAD_DIM), lambda j, i: (i % seq_tiles, 0)),
        ],
        out_specs=pl.BlockSpec((tm, tn), lambda j, i: (i, j)),
        out_shape=jax.ShapeDtypeStruct((t, QK_WIDTH), BF16),
        scratch_shapes=[pltpu.VMEM((d, tn), BF16)],
        compiler_params=_params(("parallel", "arbitrary")),
        name="k_proj",
    )(xn, w, ra, rb)


def _qvt_kernel(xn_ref, w_ref, rat_ref, rbt_ref, o_ref, wt_bf_ref, *, tn):
    j = pl.program_id(0)
    n_q_tiles = QK_WIDTH // tn

    @pl.when(pl.program_id(1) == 0)
    def _():
        wt_bf_ref[...] = w_ref[...].T.astype(BF16)

    acc = lax.dot_general(wt_bf_ref[...], xn_ref[...], (((1,), (1,)), ((), ())),
                          preferred_element_type=F32)

    @pl.when(j < n_q_tiles)
    def _():
        scale = LOG2_E * HEAD_DIM ** -0.5
        rat = rat_ref[...] * scale
        rbt = rbt_ref[...] * scale
        half = HEAD_DIM // 2
        for c in range(tn // HEAD_DIM):
            t = acc[c * HEAD_DIM:(c + 1) * HEAD_DIM, :]
            swapped = jnp.concatenate([t[half:, :], t[:half, :]], axis=0)
            o_ref[c * HEAD_DIM:(c + 1) * HEAD_DIM, :] = (t * rat + swapped * rbt).astype(BF16)

    @pl.when(j >= n_q_tiles)
    def _():
        o_ref[...] = acc.astype(BF16)


def _qvt_proj(xn, w, layer, rat, rbt, *, seq, tm=1024, tn=1024):
    t, d = xn.shape
    n = 2 * QK_WIDTH
    seq_tiles = seq // tm
    n_q_tiles = QK_WIDTH // tn
    w_col = lambda j: jnp.where(j < n_q_tiles, j, j + n_q_tiles)
    return pl.pallas_call(
        functools.partial(_qvt_kernel, tn=tn),
        grid=(n // tn, t // tm),
        in_specs=[
            pl.BlockSpec((tm, d), lambda j, i: (i, 0)),
            pl.BlockSpec((None, d, tn), lambda j, i: (layer, 0, w_col(j))),
            pl.BlockSpec((HEAD_DIM, tm), lambda j, i: (0, i % seq_tiles)),
            pl.BlockSpec((HEAD_DIM, tm), lambda j, i: (0, i % seq_tiles)),
        ],
        out_specs=pl.BlockSpec((tn, tm), lambda j, i: (j, i)),
        out_shape=jax.ShapeDtypeStruct((n, t), BF16),
        scratch_shapes=[pltpu.VMEM((tn, d), BF16)],
        compiler_params=_params(("parallel", "arbitrary")),
        name="qvt_proj",
    )(xn, w, rat, rbt)


def _attn_kernel(qt_ref, k_ref, vt_ref, lq1_ref, lk1_ref, lq2_ref, lk2_ref, sg_ref,
                 o_ref, s0_ref, s1_ref, c0_ref, c1_ref, p0_ref, p1_ref, a0_ref, a1_ref,
                 m_ref, l_ref, acc_ref, *, tq, tk, heads, lambda_init):
    qi = pl.program_id(2)
    units = [(hh, c) for hh in range(heads) for c in range(2)]
    s_refs = (s0_ref, s1_ref)
    p_refs, alpha_refs = (p0_ref, p1_ref), (a0_ref, a1_ref)
    chunk_max_refs = (c0_ref, c1_ref)

    m_ref[1] = jnp.full(m_ref.shape[1:], MASK_VALUE, F32)
    l_ref[1] = jnp.zeros(l_ref.shape[1:], F32)
    acc_ref[1] = jnp.zeros(acc_ref.shape[1:], F32)

    def kv_start(kj):
        return pl.multiple_of(kj * tk, tk)

    def scores(kj, slot, only=None):
        for u, (hh, c) in enumerate(units):
            if only is not None and hh != only:
                continue
            start = hh * V_HEAD_DIM + c * HEAD_DIM
            part = slice(start, start + HEAD_DIM)
            s = jnp.dot(k_ref[pl.ds(kv_start(kj), tk), part], qt_ref[part, :],
                        preferred_element_type=F32)
            s_refs[slot][u] = s
            chunk_max_refs[slot][u] = jnp.max(s, axis=0, keepdims=True)

    def softmax(slot, masked, only=None):
        prev = 1 - slot
        for c, (hh, _) in enumerate(units):
            if only is not None and hh != only:
                continue
            s = s_refs[slot][c]
            if masked:
                key = lax.broadcasted_iota(jnp.int32, (tk, tq), 0)
                query = lax.broadcasted_iota(jnp.int32, (tk, tq), 1)
                s = jnp.where(key <= query, s, MASK_VALUE)
                chunk_max = jnp.max(s, axis=0, keepdims=True)
            else:
                chunk_max = chunk_max_refs[slot][c]
            m_prev = m_ref[prev, c]
            m_new = jnp.maximum(m_prev, chunk_max)
            alpha = jnp.exp2(m_prev - m_new)
            p = jnp.exp2(s - m_new)
            l_ref[slot, c] = alpha * l_ref[prev, c] + jnp.sum(p, axis=0, keepdims=True)
            m_ref[slot, c] = m_new
            alpha_refs[slot][c] = alpha
            p_refs[slot][c] = p.astype(BF16)

    def accumulate(kj, slot, only=None):
        for u, (hh, _) in enumerate(units):
            if only is not None and hh != only:
                continue
            vt = vt_ref[hh * V_HEAD_DIM:(hh + 1) * V_HEAD_DIM, pl.ds(kv_start(kj), tk)]
            pv = jnp.dot(vt, p_refs[slot][u], preferred_element_type=F32)
            acc_ref[slot, u] = alpha_refs[slot][u] * acc_ref[1 - slot, u] + pv

    def pipeline_step(kj, slot, only=None):
        scores(kj + 2, slot, only)
        softmax(1 - slot, masked=False, only=only)
        accumulate(kj, slot, only)

    scores(0, 0)

    def fill():
        scores(1, 1)
        softmax(0, masked=False)

    _run_if(qi >= 1, fill)

    steady_steps = qi - 1

    def pair(u, carry):
        for hh in range(heads):
            pipeline_step(2 * u, 0, hh)
            pipeline_step(2 * u + 1, 1, hh)
        return carry

    lax.fori_loop(0, steady_steps // 2, pair, 0)

    qi_even = qi % 2 == 0
    qi_odd = jnp.logical_not(qi_even)
    even_with_prev = jnp.logical_and(qi >= 2, qi_even)
    _run_if(even_with_prev, lambda: [pipeline_step(qi - 2, 0, hh) for hh in range(heads)])

    def drain(slot, with_prev=True):
        for hh in range(heads):
            softmax(slot, masked=True, only=hh)
            if with_prev:
                accumulate(qi - 1, 1 - slot, hh)
            accumulate(qi, slot, hh)

    _run_if(even_with_prev, lambda: drain(0))
    _run_if(qi_odd, lambda: drain(1))
    _run_if(qi == 0, lambda: drain(0, with_prev=False))

    lam = (jnp.exp(jnp.sum(lq1_ref[...] * lk1_ref[...], axis=-1, keepdims=True))
           - jnp.exp(jnp.sum(lq2_ref[...] * lk2_ref[...], axis=-1, keepdims=True))
           + lambda_init)
    gain = sg_ref[...] * (1.0 - lambda_init)
    last = qi % 2
    for hh in range(heads):
        u1, u2 = 2 * hh, 2 * hh + 1
        ot = (acc_ref[last, u1] / l_ref[last, u1]
              - lam * (acc_ref[last, u2] / l_ref[last, u2]))
        ms = jnp.mean(ot * ot, axis=0, keepdims=True)
        ot = ot * lax.rsqrt(ms + SUBLN_EPS)
        o_ref[:, hh * V_HEAD_DIM:(hh + 1) * V_HEAD_DIM] = (ot.T * gain).astype(BF16)


def _diff_attention(qvt, k, lq1, lk1, lq2, lk2, subln_g, *, batch, seq, lambda_init,
                    tq=512, heads=2):
    t = k.shape[0]
    tk = tq
    q_tiles = seq // tq
    head_groups = N_DIFF_HEADS // heads
    width = heads * V_HEAD_DIM
    units = 2 * heads
    vec = lambda a: a.reshape(1, -1)
    small = lambda width: pl.BlockSpec((1, width), lambda b, h, i: (0, 0))
    return pl.pallas_call(
        functools.partial(_attn_kernel, tq=tq, tk=tk, heads=heads, lambda_init=lambda_init),
        grid=(batch, head_groups, q_tiles),
        in_specs=[
            pl.BlockSpec((width, tq), lambda b, h, i: (h, b * q_tiles + i)),
            pl.BlockSpec((seq, width), lambda b, h, i: (b, h)),
            pl.BlockSpec((width, seq), lambda b, h, i: (head_groups + h, b)),
            small(HEAD_DIM), small(HEAD_DIM), small(HEAD_DIM), small(HEAD_DIM),
            small(V_HEAD_DIM),
        ],
        out_specs=pl.BlockSpec((tq, width), lambda b, h, i: (b * q_tiles + i, h)),
        out_shape=jax.ShapeDtypeStruct((t, N_DIFF_HEADS * V_HEAD_DIM), BF16),
        scratch_shapes=[
            pltpu.VMEM((units, tk, tq), F32), pltpu.VMEM((units, tk, tq), F32),
            pltpu.VMEM((units, 1, tq), F32), pltpu.VMEM((units, 1, tq), F32),
            pltpu.VMEM((units, tk, tq), BF16), pltpu.VMEM((units, tk, tq), BF16),
            pltpu.VMEM((units, 1, tq), F32), pltpu.VMEM((units, 1, tq), F32),
            pltpu.VMEM((2, units, 1, tq), F32),
            pltpu.VMEM((2, units, 1, tq), F32),
            pltpu.VMEM((2, units, V_HEAD_DIM, tq), F32),
        ],
        compiler_params=_params(("parallel", "parallel", "arbitrary")),
        name="diff_attention",
    )(qvt, k, qvt, vec(lq1), vec(lk1), vec(lq2), vec(lk2), vec(subln_g))


def _emit_residual(h, g_ref, out_refs, final):
    if final:
        out_refs[0][...] = _rms_norm_rows(h, g_ref[...], RMS_EPS)
    else:
        out_refs[0][...] = h
        out_refs[1][...] = _rms_norm_rows(h, g_ref[...], RMS_EPS).astype(BF16)


def _residual_out(t, d, tm, final):
    spec = pl.BlockSpec((tm, d), lambda i: (i, 0))
    if final:
        return [spec], [jax.ShapeDtypeStruct((t, d), F32)]
    return [spec, spec], [jax.ShapeDtypeStruct((t, d), F32), jax.ShapeDtypeStruct((t, d), BF16)]


def _proj_residual_kernel(a_ref, w_ref, r_ref, g_ref, *out_refs, final):
    h = r_ref[...] + jnp.dot(a_ref[...], w_ref[...], preferred_element_type=F32)
    _emit_residual(h, g_ref, out_refs, final)


def _proj_residual(a, w, layer, res, g_next, *, final=False, tm=512):
    t, k = a.shape
    n = w.shape[2]
    out_specs, out_shape = _residual_out(t, n, tm, final)
    return pl.pallas_call(
        functools.partial(_proj_residual_kernel, final=final),
        grid=(t // tm,),
        in_specs=[
            pl.BlockSpec((tm, k), lambda i: (i, 0)),
            pl.BlockSpec((None, k, n), lambda i: (layer, 0, 0), pipeline_mode=pl.Buffered(1)),
            pl.BlockSpec((tm, n), lambda i: (i, 0)),
            pl.BlockSpec((1, n), lambda i: (0, 0)),
        ],
        out_specs=out_specs,
        out_shape=out_shape,
        compiler_params=_params(("parallel",)),
        name="proj_residual",
    )(a, w, res, g_next)


def _bch_kernel(xn_ref, wb_ref, wc_ref, wu_ref, gb_ref, z_ref, wb_bf_ref, wc_bf_ref,
                wu_bf_ref):
    _cast_weights_once([(wb_ref, wb_bf_ref), (wc_ref, wc_bf_ref), (wu_ref, wu_bf_ref)])
    xn = xn_ref[...]
    gb_ref[...] = jnp.dot(xn, wb_bf_ref[...], preferred_element_type=F32).astype(BF16)
    gate_c = jnp.dot(xn, wc_bf_ref[...], preferred_element_type=F32)
    u = jnp.dot(xn, wu_bf_ref[...], preferred_element_type=F32)
    z_ref[...] = (gate_c * u).astype(BF16)


def _bch_proj(xn, w, layer, *, tm=1024, tn=512):
    t, d = xn.shape
    col_tiles = d // tn
    w_spec = lambda part: pl.BlockSpec((None, d, tn),
                                       lambda j, i: (layer, 0, part * col_tiles + j))
    out_spec = pl.BlockSpec((tm, tn), lambda j, i: (i, j))
    return pl.pallas_call(
        _bch_kernel,
        grid=(col_tiles, t // tm),
        in_specs=[pl.BlockSpec((tm, d), lambda j, i: (i, 0)), w_spec(0), w_spec(1), w_spec(2)],
        out_specs=[out_spec, out_spec],
        out_shape=[jax.ShapeDtypeStruct((t, d), BF16)] * 2,
        scratch_shapes=[pltpu.VMEM((d, tn), BF16)] * 3,
        compiler_params=_params(("parallel", "arbitrary")),
        name="bch_proj",
    )(xn, w, w, w)


def _conv_out_kernel(gb_ref, z_ref, halo_ref, cw_ref, w_ref, r_ref, g_ref, *out_refs,
                     seq_tiles, final):
    tm = z_ref.shape[0]
    z = z_ref[...].astype(F32)
    halo = jnp.where(pl.program_id(0) % seq_tiles == 0, 0.0, halo_ref[...].astype(F32))
    zext = jnp.concatenate([halo, z], axis=0)
    pad = halo.shape[0]
    cw = cw_ref[...]
    zc = cw[2:3, :] * z
    zc = zc + cw[1:2, :] * zext[pad - 1:pad - 1 + tm, :]
    zc = zc + cw[0:1, :] * zext[pad - 2:pad - 2 + tm, :]
    a = (gb_ref[...].astype(F32) * zc).astype(BF16)
    h = r_ref[...] + jnp.dot(a, w_ref[...], preferred_element_type=F32)
    _emit_residual(h, g_ref, out_refs, final)


def _conv_out_proj(gate_b, z, conv_w, w, layer, res, g_next, *, seq, final=False, tm=512):
    t, d = z.shape
    n = w.shape[2]
    halo_rows = 2 * V7X_F32_SUBLANES
    halo_per_tile = tm // halo_rows
    row_spec = pl.BlockSpec((tm, d), lambda i: (i, 0))
    out_specs, out_shape = _residual_out(t, n, tm, final)
    return pl.pallas_call(
        functools.partial(_conv_out_kernel, seq_tiles=seq // tm, final=final),
        grid=(t // tm,),
        in_specs=[
            row_spec,
            row_spec,
            pl.BlockSpec((halo_rows, d), lambda i: (jnp.maximum(i * halo_per_tile - 1, 0), 0)),
            pl.BlockSpec((None, CONV_WIDTH, d), lambda i: (layer, 0, 0)),
            pl.BlockSpec((None, d, n), lambda i: (layer, 0, 0), pipeline_mode=pl.Buffered(1)),
            pl.BlockSpec((tm, n), lambda i: (i, 0)),
            pl.BlockSpec((1, n), lambda i: (0, 0)),
        ],
        out_specs=out_specs,
        out_shape=out_shape,
        compiler_params=_params(("parallel",)),
        name="conv_out_proj",
    )(gate_b, z, z, conv_w, w, res, g_next)


def _ffn_in_kernel(xn_ref, wg_ref, wu_ref, o_ref, wg_bf_ref, wu_bf_ref):
    _cast_weights_once([(wg_ref, wg_bf_ref), (wu_ref, wu_bf_ref)])
    xn = xn_ref[...]
    gate = jnp.dot(xn, wg_bf_ref[...], preferred_element_type=F32)
    up = jnp.dot(xn, wu_bf_ref[...], preferred_element_type=F32)
    o_ref[...] = (gate * jax.nn.sigmoid(gate) * up).astype(BF16)


def _ffn_in(xn, w_gate, w_up, layer, *, tm=2048, tf=512):
    t, d = xn.shape
    f = w_gate.shape[2]
    w_spec = pl.BlockSpec((None, d, tf), lambda j, i: (layer, 0, j))
    return pl.pallas_call(
        _ffn_in_kernel,
        grid=(f // tf, t // tm),
        in_specs=[pl.BlockSpec((tm, d), lambda j, i: (i, 0)), w_spec, w_spec],
        out_specs=pl.BlockSpec((tm, tf), lambda j, i: (i, j)),
        out_shape=jax.ShapeDtypeStruct((t, f), BF16),
        scratch_shapes=[pltpu.VMEM((d, tf), BF16)] * 2,
        compiler_params=_params(("parallel", "arbitrary")),
        name="ffn_in",
    )(xn, w_gate, w_up)


def _first_norm_kernel(x_ref, g_ref, o_ref):
    o_ref[...] = _rms_norm_rows(x_ref[...], g_ref[...], RMS_EPS).astype(BF16)


def _first_norm(h, g, *, tm=1024):
    t, d = h.shape
    return pl.pallas_call(
        _first_norm_kernel,
        grid=(t // tm,),
        in_specs=[pl.BlockSpec((tm, d), lambda i: (i, 0)),
                  pl.BlockSpec((1, d), lambda i: (0, 0))],
        out_specs=pl.BlockSpec((tm, d), lambda i: (i, 0)),
        out_shape=jax.ShapeDtypeStruct((t, d), BF16),
        compiler_params=_params(("parallel",)),
        name="first_norm",
    )(h, g)


def _rope_tables(seq):
    pos = jnp.arange(seq, dtype=F32)
    inv_freq = 1.0 / (ROPE_THETA ** (jnp.arange(0, HEAD_DIM, 2, dtype=F32) / HEAD_DIM))
    ang = pos[:, None] * inv_freq[None, :]
    cos, sin = jnp.cos(ang), jnp.sin(ang)
    return (jnp.concatenate([cos, cos], axis=-1),
            jnp.concatenate([-sin, sin], axis=-1))


def kernel(x, attn_norm_g, w_qkv, w_o_attn, lambda_q1, lambda_k1, lambda_q2, lambda_k2,
           subln_g, conv_norm_g, w_bch, conv_w, w_o_conv, ffn_norm_g, w_gate, w_up,
           w_down, final_norm_g):
    batch, seq, d = x.shape
    rope_a, rope_b = _rope_tables(seq)
    rope_a_t, rope_b_t = rope_a.T, rope_b.T
    row = lambda a: a.reshape(1, -1)
    w_o_attn_b, w_o_conv_b, w_down_b = (a.astype(BF16) for a in (w_o_attn, w_o_conv, w_down))

    def mixer_norm_g(layer):
        j = layer // 2
        return row(attn_norm_g[j] if layer % 2 == 0 else conv_norm_g[j])

    h = x.reshape(batch * seq, d)
    xn = _first_norm(h, mixer_norm_g(0))
    for i in range(DEPTH):
        j = i // 2
        g_ffn = row(ffn_norm_g[i])
        if i % 2 == 0:
            lambda_init = 0.8 - 0.6 * math.exp(-0.3 * i)
            k = _k_proj(xn, w_qkv, j, rope_a, rope_b, seq=seq)
            qvt = _qvt_proj(xn, w_qkv, j, rope_a_t, rope_b_t, seq=seq)
            o = _diff_attention(qvt, k, lambda_q1[j], lambda_k1[j], lambda_q2[j],
                                lambda_k2[j], subln_g[j], batch=batch, seq=seq,
                                lambda_init=lambda_init)
            h, xn = _proj_residual(o, w_o_attn_b, j, h, g_ffn)
        else:
            gate_b, z = _bch_proj(xn, w_bch, j)
            h, xn = _conv_out_proj(gate_b, z, conv_w, w_o_conv_b, j, h, g_ffn, seq=seq)
        hid = _ffn_in(xn, w_gate, w_up, i)
        if i + 1 < DEPTH:
            h, xn = _proj_residual(hid, w_down_b, i, h, mixer_norm_g(i + 1))
        else:
            (out,) = _proj_residual(hid, w_down_b, i, h, row(final_norm_g), final=True)
    return out.reshape(batch, seq, d)
```

```python
import functools
import math

import jax
import jax.numpy as jnp
from jax import lax
from jax.experimental import pallas as pl
from jax.experimental.pallas import tpu as pltpu

D_MODEL = 2048
DEPTH = 4
N_DIFF_HEADS = 8
HEAD_DIM = 128
V_HEAD_DIM = 2 * HEAD_DIM
QK_WIDTH = 2 * N_DIFF_HEADS * HEAD_DIM
ROPE_THETA = 10000.0
CONV_WIDTH = 3
RMS_EPS = 1e-6
SUBLN_EPS = 1e-5

V7X_LANES = 128
V7X_F32_SUBLANES = 8
V7X_VMEM_BYTES = 64 * 1024 * 1024
VMEM_LIMIT_BYTES = V7X_VMEM_BYTES - 6 * 1024 * 1024

MASK_VALUE = -1e30
LOG2_E = math.log2(math.e)

F32 = jnp.float32
BF16 = jnp.bfloat16


def _params(semantics):
    return pltpu.CompilerParams(dimension_semantics=semantics,
                                vmem_limit_bytes=VMEM_LIMIT_BYTES)


def _run_if(cond, fn):
    def body(_, carry):
        fn()
        return carry
    lax.fori_loop(0, cond.astype(jnp.int32), body, 0)


def _cast_weights_once(pairs):
    @pl.when(pl.program_id(1) == 0)
    def _():
        for src_ref, dst_ref in pairs:
            dst_ref[...] = src_ref[...].astype(BF16)


def _rms_norm_rows(x, g, eps):
    ms = jnp.mean(x * x, axis=-1, keepdims=True)
    return x * lax.rsqrt(ms + eps) * g


def _k_kernel(xn_ref, w_ref, ra_ref, rb_ref, o_ref, w_bf_ref, *, tn):
    _cast_weights_once([(w_ref, w_bf_ref)])
    acc = jnp.dot(xn_ref[...], w_bf_ref[...], preferred_element_type=F32)
    ra = ra_ref[...]
    rb = rb_ref[...]
    for c in range(tn // HEAD_DIM):
        t = acc[:, c * HEAD_DIM:(c + 1) * HEAD_DIM]
        r = t * ra + pltpu.roll(t, HEAD_DIM // 2, axis=1) * rb
        o_ref[:, c * HEAD_DIM:(c + 1) * HEAD_DIM] = r.astype(BF16)


def _k_proj(xn, w, layer, ra, rb, *, seq, tm=1024, tn=1024):
    t, d = xn.shape
    seq_tiles = seq // tm
    k_tiles = QK_WIDTH // tn
    return pl.pallas_call(
        functools.partial(_k_kernel, tn=tn),
        grid=(k_tiles, t // tm),
        in_specs=[
            pl.BlockSpec((tm, d), lambda j, i: (i, 0)),
            pl.BlockSpec((None, d, tn), lambda j, i: (layer, 0, k_tiles + j)),
            pl.BlockSpec((tm, HEAD_DIM), lambda j, i: (i % seq_tiles, 0)),
            pl.BlockSpec((tm, HEAD_DIM), lambda j, i: (i % seq_tiles, 0)),
        ],
        out_specs=pl.BlockSpec((tm, tn), lambda j, i: (i, j)),
        out_shape=jax.ShapeDtypeStruct((t, QK_WIDTH), BF16),
        scratch_shapes=[pltpu.VMEM((d, tn), BF16)],
        compiler_params=_params(("parallel", "arbitrary")),
        name="k_proj",
    )(xn, w, ra, rb)


def _qvt_kernel(xn_ref, w_ref, rat_ref, rbt_ref, o_ref, wt_bf_ref, *, tn):
    j = pl.program_id(0)
    n_q_tiles = QK_WIDTH // tn

    @pl.when(pl.program_id(1) == 0)
    def _():
        wt_bf_ref[...] = w_ref[...].T.astype(BF16)

    acc = lax.dot_general(wt_bf_ref[...], xn_ref[...], (((1,), (1,)), ((), ())),
                          preferred_element_type=F32)

    @pl.when(j < n_q_tiles)
    def _():
        scale = LOG2_E * HEAD_DIM ** -0.5
        rat = rat_ref[...] * scale
        rbt = rbt_ref[...] * scale
        half = HEAD_DIM // 2
        for c in range(tn // HEAD_DIM):
            t = acc[c * HEAD_DIM:(c + 1) * HEAD_DIM, :]
            swapped = jnp.concatenate([t[half:, :], t[:half, :]], axis=0)
            o_ref[c * HEAD_DIM:(c + 1) * HEAD_DIM, :] = (t * rat + swapped * rbt).astype(BF16)

    @pl.when(j >= n_q_tiles)
    def _():
        o_ref[...] = acc.astype(BF16)


def _qvt_proj(xn, w, layer, rat, rbt, *, seq, tm=1024, tn=1024):
    t, d = xn.shape
    n = 2 * QK_WIDTH
    seq_tiles = seq // tm
    n_q_tiles = QK_WIDTH // tn
    w_col = lambda j: jnp.where(j < n_q_tiles, j, j + n_q_tiles)
    return pl.pallas_call(
        functools.partial(_qvt_kernel, tn=tn),
        grid=(n // tn, t // tm),
        in_specs=[
            pl.BlockSpec((tm, d), lambda j, i: (i, 0)),
            pl.BlockSpec((None, d, tn), lambda j, i: (layer, 0, w_col(j))),
            pl.BlockSpec((HEAD_DIM, tm), lambda j, i: (0, i % seq_tiles)),
            pl.BlockSpec((HEAD_DIM, tm), lambda j, i: (0, i % seq_tiles)),
        ],
        out_specs=pl.BlockSpec((tn, tm), lambda j, i: (j, i)),
        out_shape=jax.ShapeDtypeStruct((n, t), BF16),
        scratch_shapes=[pltpu.VMEM((tn, d), BF16)],
        compiler_params=_params(("parallel", "arbitrary")),
        name="qvt_proj",
    )(xn, w, rat, rbt)


def _attn_kernel(*refs, q_tiles, **tile_params):
    acc_ref = refs[-1]
    acc_ref[1] = jnp.zeros(acc_ref.shape[1:], F32)

    def tile(qi, carry):
        _attn_tile(qi, *refs, **tile_params)
        return carry

    lax.fori_loop(0, q_tiles, tile, 0)


def _attn_tile(qi, qt_ref, k_ref, vt_ref, lq1_ref, lk1_ref, lq2_ref, lk2_ref, sg_ref,
               o_ref, s0_ref, s1_ref, c0_ref, c1_ref, p0_ref, p1_ref, a0_ref, a1_ref,
               m_ref, l_ref, acc_ref, *, tq, tk, heads, lambda_init):
    q_start = pl.multiple_of(qi * tq, tq)
    units = [(hh, c) for hh in range(heads) for c in range(2)]
    s_refs = (s0_ref, s1_ref)
    p_refs, alpha_refs = (p0_ref, p1_ref), (a0_ref, a1_ref)
    chunk_max_refs = (c0_ref, c1_ref)

    m_ref[1] = jnp.full(m_ref.shape[1:], MASK_VALUE, F32)
    l_ref[1] = jnp.zeros(l_ref.shape[1:], F32)

    def kv_start(kj):
        return pl.multiple_of(kj * tk, tk)

    def scores(kj, slot, only=None):
        for u, (hh, c) in enumerate(units):
            if only is not None and hh != only:
                continue
            start = hh * V_HEAD_DIM + c * HEAD_DIM
            part = slice(start, start + HEAD_DIM)
            s = jnp.dot(k_ref[pl.ds(kv_start(kj), tk), part],
                        qt_ref[part, pl.ds(q_start, tq)],
                        preferred_element_type=F32)
            s_refs[slot][u] = s
            chunk_max_refs[slot][u] = jnp.max(s, axis=0, keepdims=True)

    def softmax(slot, masked, only=None):
        prev = 1 - slot
        for c, (hh, _) in enumerate(units):
            if only is not None and hh != only:
                continue
            s = s_refs[slot][c]
            if masked:
                key = lax.broadcasted_iota(jnp.int32, (tk, tq), 0)
                query = lax.broadcasted_iota(jnp.int32, (tk, tq), 1)
                s = jnp.where(key <= query, s, MASK_VALUE)
                chunk_max = jnp.max(s, axis=0, keepdims=True)
            else:
                chunk_max = chunk_max_refs[slot][c]
            m_prev = m_ref[prev, c]
            m_new = jnp.maximum(m_prev, chunk_max)
            alpha = jnp.exp2(m_prev - m_new)
            p = jnp.exp2(s - m_new)
            l_ref[slot, c] = alpha * l_ref[prev, c] + jnp.sum(p, axis=0, keepdims=True)
            m_ref[slot, c] = m_new
            alpha_refs[slot][c] = alpha
            p_refs[slot][c] = p.astype(BF16)

    def accumulate(kj, slot, only=None):
        for u, (hh, _) in enumerate(units):
            if only is not None and hh != only:
                continue
            vt = vt_ref[hh * V_HEAD_DIM:(hh + 1) * V_HEAD_DIM, pl.ds(kv_start(kj), tk)]
            pv = jnp.dot(vt, p_refs[slot][u], preferred_element_type=F32)
            acc_ref[slot, u] = alpha_refs[slot][u] * acc_ref[1 - slot, u] + pv

    def pipeline_step(kj, slot, only=None):
        scores(kj + 2, slot, only)
        softmax(1 - slot, masked=False, only=only)
        accumulate(kj, slot, only)

    scores(0, 0)

    def fill():
        scores(1, 1)
        softmax(0, masked=False)

    _run_if(qi >= 1, fill)

    steady_steps = qi - 1

    def pair(u, carry):
        for hh in range(heads):
            pipeline_step(2 * u, 0, hh)
            pipeline_step(2 * u + 1, 1, hh)
        return carry

    lax.fori_loop(0, steady_steps // 2, pair, 0)

    qi_even = qi % 2 == 0
    qi_odd = jnp.logical_not(qi_even)
    even_with_prev = jnp.logical_and(qi >= 2, qi_even)
    _run_if(even_with_prev, lambda: [pipeline_step(qi - 2, 0, hh) for hh in range(heads)])

    def drain(slot, with_prev=True):
        for hh in range(heads):
            softmax(slot, masked=True, only=hh)
            if with_prev:
                accumulate(qi - 1, 1 - slot, hh)
            accumulate(qi, slot, hh)

    _run_if(even_with_prev, lambda: drain(0))
    _run_if(qi_odd, lambda: drain(1))
    _run_if(qi == 0, lambda: drain(0, with_prev=False))

    lam = (jnp.exp(jnp.sum(lq1_ref[...] * lk1_ref[...], axis=-1, keepdims=True))
           - jnp.exp(jnp.sum(lq2_ref[...] * lk2_ref[...], axis=-1, keepdims=True))
           + lambda_init)
    gain = sg_ref[...] * (1.0 - lambda_init)
    last = qi % 2
    for hh in range(heads):
        u1, u2 = 2 * hh, 2 * hh + 1
        ot = (acc_ref[last, u1] / l_ref[last, u1]
              - lam * (acc_ref[last, u2] / l_ref[last, u2]))
        ms = jnp.mean(ot * ot, axis=0, keepdims=True)
        ot = ot * lax.rsqrt(ms + SUBLN_EPS)
        o_ref[pl.ds(q_start, tq), hh * V_HEAD_DIM:(hh + 1) * V_HEAD_DIM] = (
            ot.T * gain).astype(BF16)


def _diff_attention(qvt, k, lq1, lk1, lq2, lk2, subln_g, *, batch, seq, lambda_init,
                    tq=512, heads=2):
    t = k.shape[0]
    tk = tq
    q_tiles = seq // tq
    head_groups = N_DIFF_HEADS // heads
    width = heads * V_HEAD_DIM
    units = 2 * heads
    vec = lambda a: a.reshape(1, -1)
    small = lambda width: pl.BlockSpec((1, width), lambda b, h: (0, 0))
    return pl.pallas_call(
        functools.partial(_attn_kernel, tq=tq, tk=tk, q_tiles=q_tiles, heads=heads,
                          lambda_init=lambda_init),
        grid=(batch, head_groups),
        in_specs=[
            pl.BlockSpec((width, seq), lambda b, h: (h, b)),
            pl.BlockSpec((seq, width), lambda b, h: (b, h)),
            pl.BlockSpec((width, seq), lambda b, h: (head_groups + h, b)),
            small(HEAD_DIM), small(HEAD_DIM), small(HEAD_DIM), small(HEAD_DIM),
            small(V_HEAD_DIM),
        ],
        out_specs=pl.BlockSpec((seq, width), lambda b, h: (b, h)),
        out_shape=jax.ShapeDtypeStruct((t, N_DIFF_HEADS * V_HEAD_DIM), BF16),
        scratch_shapes=[
            pltpu.VMEM((units, tk, tq), F32), pltpu.VMEM((units, tk, tq), F32),
            pltpu.VMEM((units, 1, tq), F32), pltpu.VMEM((units, 1, tq), F32),
            pltpu.VMEM((units, tk, tq), BF16), pltpu.VMEM((units, tk, tq), BF16),
            pltpu.VMEM((units, 1, tq), F32), pltpu.VMEM((units, 1, tq), F32),
            pltpu.VMEM((2, units, 1, tq), F32),
            pltpu.VMEM((2, units, 1, tq), F32),
            pltpu.VMEM((2, units, V_HEAD_DIM, tq), F32),
        ],
        compiler_params=_params(("parallel", "parallel")),
        name="diff_attention",
    )(qvt, k, qvt, vec(lq1), vec(lk1), vec(lq2), vec(lk2), vec(subln_g))


def _emit_residual(h, g_ref, out_refs, final):
    if final:
        out_refs[0][...] = _rms_norm_rows(h, g_ref[...], RMS_EPS)
    else:
        out_refs[0][...] = h
        out_refs[1][...] = _rms_norm_rows(h, g_ref[...], RMS_EPS).astype(BF16)


def _residual_out(t, d, tm, final):
    spec = pl.BlockSpec((tm, d), lambda i: (i, 0))
    if final:
        return [spec], [jax.ShapeDtypeStruct((t, d), F32)]
    return [spec, spec], [jax.ShapeDtypeStruct((t, d), F32), jax.ShapeDtypeStruct((t, d), BF16)]


def _proj_residual_kernel(a_ref, w_ref, r_ref, g_ref, *out_refs, final):
    h = r_ref[...] + jnp.dot(a_ref[...], w_ref[...], preferred_element_type=F32)
    _emit_residual(h, g_ref, out_refs, final)


def _proj_residual(a, w, layer, res, g_next, *, final=False, tm=512):
    t, k = a.shape
    n = w.shape[2]
    out_specs, out_shape = _residual_out(t, n, tm, final)
    return pl.pallas_call(
        functools.partial(_proj_residual_kernel, final=final),
        grid=(t // tm,),
        in_specs=[
            pl.BlockSpec((tm, k), lambda i: (i, 0)),
            pl.BlockSpec((None, k, n), lambda i: (layer, 0, 0), pipeline_mode=pl.Buffered(1)),
            pl.BlockSpec((tm, n), lambda i: (i, 0)),
            pl.BlockSpec((1, n), lambda i: (0, 0)),
        ],
        out_specs=out_specs,
        out_shape=out_shape,
        compiler_params=_params(("parallel",)),
        name="proj_residual",
    )(a, w, res, g_next)


def _bch_kernel(xn_ref, wb_ref, wc_ref, wu_ref, gb_ref, z_ref, wb_bf_ref, wc_bf_ref,
                wu_bf_ref):
    _cast_weights_once([(wb_ref, wb_bf_ref), (wc_ref, wc_bf_ref), (wu_ref, wu_bf_ref)])
    xn = xn_ref[...]
    gb_ref[...] = jnp.dot(xn, wb_bf_ref[...], preferred_element_type=F32).astype(BF16)
    gate_c = jnp.dot(xn, wc_bf_ref[...], preferred_element_type=F32)
    u = jnp.dot(xn, wu_bf_ref[...], preferred_element_type=F32)
    z_ref[...] = (gate_c * u).astype(BF16)


def _bch_proj(xn, w, layer, *, tm=1024, tn=512):
    t, d = xn.shape
    col_tiles = d // tn
    w_spec = lambda part: pl.BlockSpec((None, d, tn),
                                       lambda j, i: (layer, 0, part * col_tiles + j))
    out_spec = pl.BlockSpec((tm, tn), lambda j, i: (i, j))
    return pl.pallas_call(
        _bch_kernel,
        grid=(col_tiles, t // tm),
        in_specs=[pl.BlockSpec((tm, d), lambda j, i: (i, 0)), w_spec(0), w_spec(1), w_spec(2)],
        out_specs=[out_spec, out_spec],
        out_shape=[jax.ShapeDtypeStruct((t, d), BF16)] * 2,
        scratch_shapes=[pltpu.VMEM((d, tn), BF16)] * 3,
        compiler_params=_params(("parallel", "arbitrary")),
        name="bch_proj",
    )(xn, w, w, w)


def _conv_out_kernel(gb_ref, z_ref, halo_ref, cw_ref, w_ref, r_ref, g_ref, *out_refs,
                     seq_tiles, final):
    tm = z_ref.shape[0]
    z = z_ref[...].astype(F32)
    halo = jnp.where(pl.program_id(0) % seq_tiles == 0, 0.0, halo_ref[...].astype(F32))
    zext = jnp.concatenate([halo, z], axis=0)
    pad = halo.shape[0]
    cw = cw_ref[...]
    zc = cw[2:3, :] * z
    zc = zc + cw[1:2, :] * zext[pad - 1:pad - 1 + tm, :]
    zc = zc + cw[0:1, :] * zext[pad - 2:pad - 2 + tm, :]
    a = (gb_ref[...].astype(F32) * zc).astype(BF16)
    h = r_ref[...] + jnp.dot(a, w_ref[...], preferred_element_type=F32)
    _emit_residual(h, g_ref, out_refs, final)


def _conv_out_proj(gate_b, z, conv_w, w, layer, res, g_next, *, seq, final=False, tm=512):
    t, d = z.shape
    n = w.shape[2]
    halo_rows = 2 * V7X_F32_SUBLANES
    halo_per_tile = tm // halo_rows
    row_spec = pl.BlockSpec((tm, d), lambda i: (i, 0))
    out_specs, out_shape = _residual_out(t, n, tm, final)
    return pl.pallas_call(
        functools.partial(_conv_out_kernel, seq_tiles=seq // tm, final=final),
        grid=(t // tm,),
        in_specs=[
            row_spec,
            row_spec,
            pl.BlockSpec((halo_rows, d), lambda i: (jnp.maximum(i * halo_per_tile - 1, 0), 0)),
            pl.BlockSpec((None, CONV_WIDTH, d), lambda i: (layer, 0, 0)),
            pl.BlockSpec((None, d, n), lambda i: (layer, 0, 0), pipeline_mode=pl.Buffered(1)),
            pl.BlockSpec((tm, n), lambda i: (i, 0)),
            pl.BlockSpec((1, n), lambda i: (0, 0)),
        ],
        out_specs=out_specs,
        out_shape=out_shape,
        compiler_params=_params(("parallel",)),
        name="conv_out_proj",
    )(gate_b, z, z, conv_w, w, res, g_next)


def _ffn_in_kernel(xn_ref, wg_ref, wu_ref, o_ref, wg_bf_ref, wu_bf_ref):
    _cast_weights_once([(wg_ref, wg_bf_ref), (wu_ref, wu_bf_ref)])
    xn = xn_ref[...]
    gate = jnp.dot(xn, wg_bf_ref[...], preferred_element_type=F32)
    up = jnp.dot(xn, wu_bf_ref[...], preferred_element_type=F32)
    o_ref[...] = (gate * jax.nn.sigmoid(gate) * up).astype(BF16)


def _ffn_in(xn, w_gate, w_up, layer, *, tm=1024, tf=512):
    t, d = xn.shape
    f = w_gate.shape[2]
    w_spec = pl.BlockSpec((None, d, tf), lambda j, i: (layer, 0, j))
    return pl.pallas_call(
        _ffn_in_kernel,
        grid=(f // tf, t // tm),
        in_specs=[pl.BlockSpec((tm, d), lambda j, i: (i, 0)), w_spec, w_spec],
        out_specs=pl.BlockSpec((tm, tf), lambda j, i: (i, j)),
        out_shape=jax.ShapeDtypeStruct((t, f), BF16),
        scratch_shapes=[pltpu.VMEM((d, tf), BF16)] * 2,
        compiler_params=_params(("parallel", "arbitrary")),
        name="ffn_in",
    )(xn, w_gate, w_up)


def _first_norm_kernel(x_ref, g_ref, o_ref):
    o_ref[...] = _rms_norm_rows(x_ref[...], g_ref[...], RMS_EPS).astype(BF16)


def _first_norm(h, g, *, tm=1024):
    t, d = h.shape
    return pl.pallas_call(
        _first_norm_kernel,
        grid=(t // tm,),
        in_specs=[pl.BlockSpec((tm, d), lambda i: (i, 0)),
                  pl.BlockSpec((1, d), lambda i: (0, 0))],
        out_specs=pl.BlockSpec((tm, d), lambda i: (i, 0)),
        out_shape=jax.ShapeDtypeStruct((t, d), BF16),
        compiler_params=_params(("parallel",)),
        name="first_norm",
    )(h, g)


def _rope_tables(seq):
    pos = jnp.arange(seq, dtype=F32)
    inv_freq = 1.0 / (ROPE_THETA ** (jnp.arange(0, HEAD_DIM, 2, dtype=F32) / HEAD_DIM))
    ang = pos[:, None] * inv_freq[None, :]
    cos, sin = jnp.cos(ang), jnp.sin(ang)
    return (jnp.concatenate([cos, cos], axis=-1),
            jnp.concatenate([-sin, sin], axis=-1))


def kernel(x, attn_norm_g, w_qkv, w_o_attn, lambda_q1, lambda_k1, lambda_q2, lambda_k2,
           subln_g, conv_norm_g, w_bch, conv_w, w_o_conv, ffn_norm_g, w_gate, w_up,
           w_down, final_norm_g):
    batch, seq, d = x.shape
    rope_a, rope_b = _rope_tables(seq)
    rope_a_t, rope_b_t = rope_a.T, rope_b.T
    row = lambda a: a.reshape(1, -1)
    w_o_attn_b, w_o_conv_b, w_down_b = (a.astype(BF16) for a in (w_o_attn, w_o_conv, w_down))

    def mixer_norm_g(layer):
        j = layer // 2
        return row(attn_norm_g[j] if layer % 2 == 0 else conv_norm_g[j])

    h = x.reshape(batch * seq, d)
    xn = _first_norm(h, mixer_norm_g(0))
    for i in range(DEPTH):
        j = i // 2
        g_ffn = row(ffn_norm_g[i])
        if i % 2 == 0:
            lambda_init = 0.8 - 0.6 * math.exp(-0.3 * i)
            k = _k_proj(xn, w_qkv, j, rope_a, rope_b, seq=seq)
            qvt = _qvt_proj(xn, w_qkv, j, rope_a_t, rope_b_t, seq=seq)
            o = _diff_attention(qvt, k, lambda_q1[j], lambda_k1[j], lambda_q2[j],
                                lambda_k2[j], subln_g[j], batch=batch, seq=seq,
                                lambda_init=lambda_init)
            h, xn = _proj_residual(o, w_o_attn_b, j, h, g_ffn)
        else:
            gate_b, z = _bch_proj(xn, w_bch, j)
            h, xn = _conv_out_proj(gate_b, z, conv_w, w_o_conv_b, j, h, g_ffn, seq=seq)
        hid = _ffn_in(xn, w_gate, w_up, i)
        if i + 1 < DEPTH:
            h, xn = _proj_residual(hid, w_down_b, i, h, mixer_norm_g(i + 1))
        else:
            (out,) = _proj_residual(hid, w_down_b, i, h, row(final_norm_g), final=True)
    return out.reshape(batch, seq, d)
```

```python
import functools
import math

import jax
import jax.numpy as jnp
from jax import lax
from jax.experimental import pallas as pl
from jax.experimental.pallas import tpu as pltpu

D_MODEL = 2048
DEPTH = 4
N_DIFF_HEADS = 8
HEAD_DIM = 128
V_HEAD_DIM = 2 * HEAD_DIM
QK_WIDTH = 2 * N_DIFF_HEADS * HEAD_DIM
ROPE_THETA = 10000.0
CONV_WIDTH = 3
RMS_EPS = 1e-6
SUBLN_EPS = 1e-5

V7X_LANES = 128
V7X_F32_SUBLANES = 8
V7X_VMEM_BYTES = 64 * 1024 * 1024
VMEM_LIMIT_BYTES = V7X_VMEM_BYTES - 6 * 1024 * 1024

MASK_VALUE = -1e30
LOG2_E = math.log2(math.e)

F32 = jnp.float32
BF16 = jnp.bfloat16


def _params(semantics):
    return pltpu.CompilerParams(dimension_semantics=semantics,
                                vmem_limit_bytes=VMEM_LIMIT_BYTES)


def _run_if(cond, fn):
    def body(_, carry):
        fn()
        return carry
    lax.fori_loop(0, jnp.asarray(cond, jnp.int32), body, 0)


def _cast_weights_once(pairs):
    @pl.when(pl.program_id(1) == 0)
    def _():
        for src_ref, dst_ref in pairs:
            dst_ref[...] = src_ref[...].astype(BF16)


def _rms_norm_rows(x, g, eps):
    ms = jnp.mean(x * x, axis=-1, keepdims=True)
    return x * lax.rsqrt(ms + eps) * g


def _k_kernel(xn_ref, w_ref, ra_ref, rb_ref, o_ref, w_bf_ref, *, tn):
    _cast_weights_once([(w_ref, w_bf_ref)])
    acc = jnp.dot(xn_ref[...], w_bf_ref[...], preferred_element_type=F32)
    ra = ra_ref[...]
    rb = rb_ref[...]
    for c in range(tn // HEAD_DIM):
        t = acc[:, c * HEAD_DIM:(c + 1) * HEAD_DIM]
        r = t * ra + pltpu.roll(t, HEAD_DIM // 2, axis=1) * rb
        o_ref[:, c * HEAD_DIM:(c + 1) * HEAD_DIM] = r.astype(BF16)


def _k_proj(xn, w, layer, ra, rb, *, seq, tm=1024, tn=1024):
    t, d = xn.shape
    seq_tiles = seq // tm
    k_tiles = QK_WIDTH // tn
    return pl.pallas_call(
        functools.partial(_k_kernel, tn=tn),
        grid=(k_tiles, t // tm),
        in_specs=[
            pl.BlockSpec((tm, d), lambda j, i: (i, 0)),
            pl.BlockSpec((None, d, tn), lambda j, i: (layer, 0, k_tiles + j)),
            pl.BlockSpec((tm, HEAD_DIM), lambda j, i: (i % seq_tiles, 0)),
            pl.BlockSpec((tm, HEAD_DIM), lambda j, i: (i % seq_tiles, 0)),
        ],
        out_specs=pl.BlockSpec((tm, tn), lambda j, i: (i, j)),
        out_shape=jax.ShapeDtypeStruct((t, QK_WIDTH), BF16),
        scratch_shapes=[pltpu.VMEM((d, tn), BF16)],
        compiler_params=_params(("parallel", "arbitrary")),
        name="k_proj",
    )(xn, w, ra, rb)


def _qvt_kernel(xn_ref, w_ref, rat_ref, rbt_ref, o_ref, wt_bf_ref, *, tn):
    j = pl.program_id(0)
    n_q_tiles = QK_WIDTH // tn

    @pl.when(pl.program_id(1) == 0)
    def _():
        wt_bf_ref[...] = w_ref[...].T.astype(BF16)

    acc = lax.dot_general(wt_bf_ref[...], xn_ref[...], (((1,), (1,)), ((), ())),
                          preferred_element_type=F32)

    @pl.when(j < n_q_tiles)
    def _():
        scale = LOG2_E * HEAD_DIM ** -0.5
        rat = rat_ref[...] * scale
        rbt = rbt_ref[...] * scale
        half = HEAD_DIM // 2
        for c in range(tn // HEAD_DIM):
            t = acc[c * HEAD_DIM:(c + 1) * HEAD_DIM, :]
            swapped = jnp.concatenate([t[half:, :], t[:half, :]], axis=0)
            o_ref[c * HEAD_DIM:(c + 1) * HEAD_DIM, :] = (t * rat + swapped * rbt).astype(BF16)

    @pl.when(j >= n_q_tiles)
    def _():
        o_ref[...] = acc.astype(BF16)


def _qvt_proj(xn, w, layer, rat, rbt, *, seq, tm=1024, tn=1024):
    t, d = xn.shape
    n = 2 * QK_WIDTH
    seq_tiles = seq // tm
    n_q_tiles = QK_WIDTH // tn
    w_col = lambda j: jnp.where(j < n_q_tiles, j, j + n_q_tiles)
    return pl.pallas_call(
        functools.partial(_qvt_kernel, tn=tn),
        grid=(n // tn, t // tm),
        in_specs=[
            pl.BlockSpec((tm, d), lambda j, i: (i, 0)),
            pl.BlockSpec((None, d, tn), lambda j, i: (layer, 0, w_col(j))),
            pl.BlockSpec((HEAD_DIM, tm), lambda j, i: (0, i % seq_tiles)),
            pl.BlockSpec((HEAD_DIM, tm), lambda j, i: (0, i % seq_tiles)),
        ],
        out_specs=pl.BlockSpec((tn, tm), lambda j, i: (j, i)),
        out_shape=jax.ShapeDtypeStruct((n, t), BF16),
        scratch_shapes=[pltpu.VMEM((tn, d), BF16)],
        compiler_params=_params(("parallel", "arbitrary")),
        name="qvt_proj",
    )(xn, w, rat, rbt)


def _attn_kernel(*refs, q_tiles, **tile_params):
    acc_ref = refs[-1]
    acc_ref[1] = jnp.zeros(acc_ref.shape[1:], F32)
    _attn_tile(0, *refs, q_tiles=q_tiles, first_scores_only=True, **tile_params)

    def tile(qi, carry):
        _attn_tile(qi, *refs, q_tiles=q_tiles, **tile_params)
        return carry

    lax.fori_loop(0, q_tiles, tile, 0)


def _attn_tile(qi, qt_ref, k_ref, vt_ref, lq1_ref, lk1_ref, lq2_ref, lk2_ref, sg_ref,
               o_ref, s0_ref, s1_ref, c0_ref, c1_ref, p0_ref, p1_ref, a0_ref, a1_ref,
               m_ref, l_ref, acc_ref, *, tq, tk, q_tiles, heads, lambda_init,
               first_scores_only=False):
    q_start = pl.multiple_of(qi * tq, tq)
    units = [(hh, c) for hh in range(heads) for c in range(2)]
    s_refs = (s0_ref, s1_ref)
    p_refs, alpha_refs = (p0_ref, p1_ref), (a0_ref, a1_ref)
    chunk_max_refs = (c0_ref, c1_ref)

    m_ref[1] = jnp.full(m_ref.shape[1:], MASK_VALUE, F32)
    l_ref[1] = jnp.zeros(l_ref.shape[1:], F32)

    def kv_start(kj):
        return pl.multiple_of(kj * tk, tk)

    def scores(kj, slot, only=None, tile_start=q_start):
        for u, (hh, c) in enumerate(units):
            if only is not None and hh != only:
                continue
            start = hh * V_HEAD_DIM + c * HEAD_DIM
            part = slice(start, start + HEAD_DIM)
            s = jnp.dot(k_ref[pl.ds(kv_start(kj), tk), part],
                        qt_ref[part, pl.ds(tile_start, tq)],
                        preferred_element_type=F32)
            s_refs[slot][u] = s
            chunk_max_refs[slot][u] = jnp.max(s, axis=0, keepdims=True)

    def softmax(slot, masked, only=None):
        prev = 1 - slot
        for c, (hh, _) in enumerate(units):
            if only is not None and hh != only:
                continue
            s = s_refs[slot][c]
            if masked:
                key = lax.broadcasted_iota(jnp.int32, (tk, tq), 0)
                query = lax.broadcasted_iota(jnp.int32, (tk, tq), 1)
                s = jnp.where(key <= query, s, MASK_VALUE)
                chunk_max = jnp.max(s, axis=0, keepdims=True)
            else:
                chunk_max = chunk_max_refs[slot][c]
            m_prev = m_ref[prev, c]
            m_new = jnp.maximum(m_prev, chunk_max)
            alpha = jnp.exp2(m_prev - m_new)
            p = jnp.exp2(s - m_new)
            l_ref[slot, c] = alpha * l_ref[prev, c] + jnp.sum(p, axis=0, keepdims=True)
            m_ref[slot, c] = m_new
            alpha_refs[slot][c] = alpha
            p_refs[slot][c] = p.astype(BF16)

    def accumulate(kj, slot, only=None):
        for u, (hh, _) in enumerate(units):
            if only is not None and hh != only:
                continue
            vt = vt_ref[hh * V_HEAD_DIM:(hh + 1) * V_HEAD_DIM, pl.ds(kv_start(kj), tk)]
            pv = jnp.dot(vt, p_refs[slot][u], preferred_element_type=F32)
            acc_ref[slot, u] = alpha_refs[slot][u] * acc_ref[1 - slot, u] + pv

    def pipeline_step(kj, slot, only=None):
        scores(kj + 2, slot, only)
        softmax(1 - slot, masked=False, only=only)
        accumulate(kj, slot, only)

    if first_scores_only:
        scores(0, 0)
        return

    def fill():
        scores(1, 1)
        softmax(0, masked=False)

    _run_if(qi >= 1, fill)

    steady_steps = qi - 1

    def pair(u, carry):
        for hh in range(heads):
            pipeline_step(2 * u, 0, hh)
            pipeline_step(2 * u + 1, 1, hh)
        return carry

    lax.fori_loop(0, steady_steps // 2, pair, 0)

    qi_even = qi % 2 == 0
    qi_odd = jnp.logical_not(qi_even)
    even_with_prev = jnp.logical_and(qi >= 2, qi_even)
    _run_if(even_with_prev, lambda: [pipeline_step(qi - 2, 0, hh) for hh in range(heads)])

    def drain(slot, with_prev=True):
        for hh in range(heads):
            softmax(slot, masked=True, only=hh)
            if with_prev:
                accumulate(qi - 1, 1 - slot, hh)
            accumulate(qi, slot, hh)

    _run_if(even_with_prev, lambda: drain(0))
    _run_if(qi_odd, lambda: drain(1))
    _run_if(qi == 0, lambda: drain(0, with_prev=False))

    lam = (jnp.exp(jnp.sum(lq1_ref[...] * lk1_ref[...], axis=-1, keepdims=True))
           - jnp.exp(jnp.sum(lq2_ref[...] * lk2_ref[...], axis=-1, keepdims=True))
           + lambda_init)
    gain = sg_ref[...] * (1.0 - lambda_init)
    last = qi % 2
    for hh in range(heads):
        u1, u2 = 2 * hh, 2 * hh + 1
        ot = (acc_ref[last, u1] / l_ref[last, u1]
              - lam * (acc_ref[last, u2] / l_ref[last, u2]))
        ms = jnp.mean(ot * ot, axis=0, keepdims=True)
        ot = ot * lax.rsqrt(ms + SUBLN_EPS)
        o_ref[pl.ds(q_start, tq), hh * V_HEAD_DIM:(hh + 1) * V_HEAD_DIM] = (
            ot.T * gain).astype(BF16)
    next_tile = jnp.minimum(qi + 1, q_tiles - 1)
    scores(0, 0, tile_start=pl.multiple_of(next_tile * tq, tq))


def _diff_attention(qvt, k, lq1, lk1, lq2, lk2, subln_g, *, batch, seq, lambda_init,
                    tq=512, heads=2):
    t = k.shape[0]
    tk = tq
    q_tiles = seq // tq
    head_groups = N_DIFF_HEADS // heads
    width = heads * V_HEAD_DIM
    units = 2 * heads
    vec = lambda a: a.reshape(1, -1)
    small = lambda width: pl.BlockSpec((1, width), lambda b, h: (0, 0))
    return pl.pallas_call(
        functools.partial(_attn_kernel, tq=tq, tk=tk, q_tiles=q_tiles, heads=heads,
                          lambda_init=lambda_init),
        grid=(batch, head_groups),
        in_specs=[
            pl.BlockSpec((width, seq), lambda b, h: (h, b)),
            pl.BlockSpec((seq, width), lambda b, h: (b, h)),
            pl.BlockSpec((width, seq), lambda b, h: (head_groups + h, b)),
            small(HEAD_DIM), small(HEAD_DIM), small(HEAD_DIM), small(HEAD_DIM),
            small(V_HEAD_DIM),
        ],
        out_specs=pl.BlockSpec((seq, width), lambda b, h: (b, h)),
        out_shape=jax.ShapeDtypeStruct((t, N_DIFF_HEADS * V_HEAD_DIM), BF16),
        scratch_shapes=[
            pltpu.VMEM((units, tk, tq), F32), pltpu.VMEM((units, tk, tq), F32),
            pltpu.VMEM((units, 1, tq), F32), pltpu.VMEM((units, 1, tq), F32),
            pltpu.VMEM((units, tk, tq), BF16), pltpu.VMEM((units, tk, tq), BF16),
            pltpu.VMEM((units, 1, tq), F32), pltpu.VMEM((units, 1, tq), F32),
            pltpu.VMEM((2, units, 1, tq), F32),
            pltpu.VMEM((2, units, 1, tq), F32),
            pltpu.VMEM((2, units, V_HEAD_DIM, tq), F32),
        ],
        compiler_params=_params(("parallel", "parallel")),
        name="diff_attention",
    )(qvt, k, qvt, vec(lq1), vec(lk1), vec(lq2), vec(lk2), vec(subln_g))


def _emit_residual(h, g_ref, out_refs, final):
    if final:
        out_refs[0][...] = _rms_norm_rows(h, g_ref[...], RMS_EPS)
    else:
        out_refs[0][...] = h
        out_refs[1][...] = _rms_norm_rows(h, g_ref[...], RMS_EPS).astype(BF16)


def _residual_out(t, d, tm, final):
    spec = pl.BlockSpec((tm, d), lambda i: (i, 0))
    if final:
        return [spec], [jax.ShapeDtypeStruct((t, d), F32)]
    return [spec, spec], [jax.ShapeDtypeStruct((t, d), F32), jax.ShapeDtypeStruct((t, d), BF16)]


def _proj_residual_kernel(a_ref, w_ref, r_ref, g_ref, *out_refs, final):
    h = r_ref[...] + jnp.dot(a_ref[...], w_ref[...], preferred_element_type=F32)
    _emit_residual(h, g_ref, out_refs, final)


def _proj_residual(a, w, layer, res, g_next, *, final=False, tm=512):
    t, k = a.shape
    n = w.shape[2]
    out_specs, out_shape = _residual_out(t, n, tm, final)
    return pl.pallas_call(
        functools.partial(_proj_residual_kernel, final=final),
        grid=(t // tm,),
        in_specs=[
            pl.BlockSpec((tm, k), lambda i: (i, 0)),
            pl.BlockSpec((None, k, n), lambda i: (layer, 0, 0), pipeline_mode=pl.Buffered(1)),
            pl.BlockSpec((tm, n), lambda i: (i, 0)),
            pl.BlockSpec((1, n), lambda i: (0, 0)),
        ],
        out_specs=out_specs,
        out_shape=out_shape,
        compiler_params=_params(("parallel",)),
        name="proj_residual",
    )(a, w, res, g_next)


def _bch_kernel(xn_ref, wb_ref, wc_ref, wu_ref, gb_ref, z_ref, wb_bf_ref, wc_bf_ref,
                wu_bf_ref):
    _cast_weights_once([(wb_ref, wb_bf_ref), (wc_ref, wc_bf_ref), (wu_ref, wu_bf_ref)])
    xn = xn_ref[...]
    gb_ref[...] = jnp.dot(xn, wb_bf_ref[...], preferred_element_type=F32).astype(BF16)
    gate_c = jnp.dot(xn, wc_bf_ref[...], preferred_element_type=F32)
    u = jnp.dot(xn, wu_bf_ref[...], preferred_element_type=F32)
    z_ref[...] = (gate_c * u).astype(BF16)


def _bch_proj(xn, w, layer, *, tm=1024, tn=512):
    t, d = xn.shape
    col_tiles = d // tn
    w_spec = lambda part: pl.BlockSpec((None, d, tn),
                                       lambda j, i: (layer, 0, part * col_tiles + j))
    out_spec = pl.BlockSpec((tm, tn), lambda j, i: (i, j))
    return pl.pallas_call(
        _bch_kernel,
        grid=(col_tiles, t // tm),
        in_specs=[pl.BlockSpec((tm, d), lambda j, i: (i, 0)), w_spec(0), w_spec(1), w_spec(2)],
        out_specs=[out_spec, out_spec],
        out_shape=[jax.ShapeDtypeStruct((t, d), BF16)] * 2,
        scratch_shapes=[pltpu.VMEM((d, tn), BF16)] * 3,
        compiler_params=_params(("parallel", "arbitrary")),
        name="bch_proj",
    )(xn, w, w, w)


def _conv_out_kernel(gb_ref, z_ref, halo_ref, cw_ref, w_ref, r_ref, g_ref, *out_refs,
                     seq_tiles, final):
    tm = z_ref.shape[0]
    z = z_ref[...].astype(F32)
    halo = jnp.where(pl.program_id(0) % seq_tiles == 0, 0.0, halo_ref[...].astype(F32))
    zext = jnp.concatenate([halo, z], axis=0)
    pad = halo.shape[0]
    cw = cw_ref[...]
    zc = cw[2:3, :] * z
    zc = zc + cw[1:2, :] * zext[pad - 1:pad - 1 + tm, :]
    zc = zc + cw[0:1, :] * zext[pad - 2:pad - 2 + tm, :]
    a = (gb_ref[...].astype(F32) * zc).astype(BF16)
    h = r_ref[...] + jnp.dot(a, w_ref[...], preferred_element_type=F32)
    _emit_residual(h, g_ref, out_refs, final)


def _conv_out_proj(gate_b, z, conv_w, w, layer, res, g_next, *, seq, final=False, tm=512):
    t, d = z.shape
    n = w.shape[2]
    halo_rows = 2 * V7X_F32_SUBLANES
    halo_per_tile = tm // halo_rows
    row_spec = pl.BlockSpec((tm, d), lambda i: (i, 0))
    out_specs, out_shape = _residual_out(t, n, tm, final)
    return pl.pallas_call(
        functools.partial(_conv_out_kernel, seq_tiles=seq // tm, final=final),
        grid=(t // tm,),
        in_specs=[
            row_spec,
            row_spec,
            pl.BlockSpec((halo_rows, d), lambda i: (jnp.maximum(i * halo_per_tile - 1, 0), 0)),
            pl.BlockSpec((None, CONV_WIDTH, d), lambda i: (layer, 0, 0)),
            pl.BlockSpec((None, d, n), lambda i: (layer, 0, 0), pipeline_mode=pl.Buffered(1)),
            pl.BlockSpec((tm, n), lambda i: (i, 0)),
            pl.BlockSpec((1, n), lambda i: (0, 0)),
        ],
        out_specs=out_specs,
        out_shape=out_shape,
        compiler_params=_params(("parallel",)),
        name="conv_out_proj",
    )(gate_b, z, z, conv_w, w, res, g_next)


def _ffn_in_kernel(xn_ref, wg_ref, wu_ref, o_ref, wg_bf_ref, wu_bf_ref):
    _cast_weights_once([(wg_ref, wg_bf_ref), (wu_ref, wu_bf_ref)])
    xn = xn_ref[...]
    gate = jnp.dot(xn, wg_bf_ref[...], preferred_element_type=F32)
    up = jnp.dot(xn, wu_bf_ref[...], preferred_element_type=F32)
    o_ref[...] = (gate * jax.nn.sigmoid(gate) * up).astype(BF16)


def _ffn_in(xn, w_gate, w_up, layer, *, tm=1024, tf=512):
    t, d = xn.shape
    f = w_gate.shape[2]
    w_spec = pl.BlockSpec((None, d, tf), lambda j, i: (layer, 0, j))
    return pl.pallas_call(
        _ffn_in_kernel,
        grid=(f // tf, t // tm),
        in_specs=[pl.BlockSpec((tm, d), lambda j, i: (i, 0)), w_spec, w_spec],
        out_specs=pl.BlockSpec((tm, tf), lambda j, i: (i, j)),
        out_shape=jax.ShapeDtypeStruct((t, f), BF16),
        scratch_shapes=[pltpu.VMEM((d, tf), BF16)] * 2,
        compiler_params=_params(("parallel", "arbitrary")),
        name="ffn_in",
    )(xn, w_gate, w_up)


def _first_norm_kernel(x_ref, g_ref, o_ref):
    o_ref[...] = _rms_norm_rows(x_ref[...], g_ref[...], RMS_EPS).astype(BF16)


def _first_norm(h, g, *, tm=1024):
    t, d = h.shape
    return pl.pallas_call(
        _first_norm_kernel,
        grid=(t // tm,),
        in_specs=[pl.BlockSpec((tm, d), lambda i: (i, 0)),
                  pl.BlockSpec((1, d), lambda i: (0, 0))],
        out_specs=pl.BlockSpec((tm, d), lambda i: (i, 0)),
        out_shape=jax.ShapeDtypeStruct((t, d), BF16),
        compiler_params=_params(("parallel",)),
        name="first_norm",
    )(h, g)


def _rope_tables(seq):
    pos = jnp.arange(seq, dtype=F32)
    inv_freq = 1.0 / (ROPE_THETA ** (jnp.arange(0, HEAD_DIM, 2, dtype=F32) / HEAD_DIM))
    ang = pos[:, None] * inv_freq[None, :]
    cos, sin = jnp.cos(ang), jnp.sin(ang)
    return (jnp.concatenate([cos, cos], axis=-1),
            jnp.concatenate([-sin, sin], axis=-1))


def kernel(x, attn_norm_g, w_qkv, w_o_attn, lambda_q1, lambda_k1, lambda_q2, lambda_k2,
           subln_g, conv_norm_g, w_bch, conv_w, w_o_conv, ffn_norm_g, w_gate, w_up,
           w_down, final_norm_g):
    batch, seq, d = x.shape
    rope_a, rope_b = _rope_tables(seq)
    rope_a_t, rope_b_t = rope_a.T, rope_b.T
    row = lambda a: a.reshape(1, -1)
    w_o_attn_b, w_o_conv_b, w_down_b = (a.astype(BF16) for a in (w_o_attn, w_o_conv, w_down))

    def mixer_norm_g(layer):
        j = layer // 2
        return row(attn_norm_g[j] if layer % 2 == 0 else conv_norm_g[j])

    h = x.reshape(batch * seq, d)
    xn = _first_norm(h, mixer_norm_g(0))
    for i in range(DEPTH):
        j = i // 2
        g_ffn = row(ffn_norm_g[i])
        if i % 2 == 0:
            lambda_init = 0.8 - 0.6 * math.exp(-0.3 * i)
            k = _k_proj(xn, w_qkv, j, rope_a, rope_b, seq=seq)
            qvt = _qvt_proj(xn, w_qkv, j, rope_a_t, rope_b_t, seq=seq)
            o = _diff_attention(qvt, k, lambda_q1[j], lambda_k1[j], lambda_q2[j],
                                lambda_k2[j], subln_g[j], batch=batch, seq=seq,
                                lambda_init=lambda_init)
            h, xn = _proj_residual(o, w_o_attn_b, j, h, g_ffn)
        else:
            gate_b, z = _bch_proj(xn, w_bch, j)
            h, xn = _conv_out_proj(gate_b, z, conv_w, w_o_conv_b, j, h, g_ffn, seq=seq)
        hid = _ffn_in(xn, w_gate, w_up, i)
        if i + 1 < DEPTH:
            h, xn = _proj_residual(hid, w_down_b, i, h, mixer_norm_g(i + 1))
        else:
            (out,) = _proj_residual(hid, w_down_b, i, h, row(final_norm_g), final=True)
    return out.reshape(batch, seq, d)
```

```python
import functools
import math

import jax
import jax.numpy as jnp
from jax import lax
from jax.experimental import pallas as pl
from jax.experimental.pallas import tpu as pltpu

D_MODEL = 2048
DEPTH = 4
N_DIFF_HEADS = 8
HEAD_DIM = 128
V_HEAD_DIM = 2 * HEAD_DIM
QK_WIDTH = 2 * N_DIFF_HEADS * HEAD_DIM
ROPE_THETA = 10000.0
CONV_WIDTH = 3
RMS_EPS = 1e-6
SUBLN_EPS = 1e-5

V7X_LANES = 128
V7X_F32_SUBLANES = 8
V7X_VMEM_BYTES = 64 * 1024 * 1024
VMEM_LIMIT_BYTES = V7X_VMEM_BYTES - 6 * 1024 * 1024

MASK_VALUE = -1e30
LOG2_E = math.log2(math.e)

F32 = jnp.float32
BF16 = jnp.bfloat16


def _params(semantics):
    return pltpu.CompilerParams(dimension_semantics=semantics,
                                vmem_limit_bytes=VMEM_LIMIT_BYTES)


def _run_if(cond, fn):
    def body(_, carry):
        fn()
        return carry
    lax.fori_loop(0, jnp.asarray(cond, jnp.int32), body, 0)


def _cast_weights_once(pairs, row_axis=1):
    @pl.when(pl.program_id(row_axis) == 0)
    def _():
        for src_ref, dst_ref in pairs:
            dst_ref[...] = src_ref[...].astype(BF16)


def _rms_norm_rows(x, g, eps):
    ms = jnp.mean(x * x, axis=-1, keepdims=True)
    return x * lax.rsqrt(ms + eps) * g


def _k_kernel(xn_ref, w_ref, ra_ref, rb_ref, o_ref, w_bf_ref, *, tn):
    _cast_weights_once([(w_ref, w_bf_ref)])
    acc = jnp.dot(xn_ref[...], w_bf_ref[...], preferred_element_type=F32)
    ra = ra_ref[...]
    rb = rb_ref[...]
    for c in range(tn // HEAD_DIM):
        t = acc[:, c * HEAD_DIM:(c + 1) * HEAD_DIM]
        r = t * ra + pltpu.roll(t, HEAD_DIM // 2, axis=1) * rb
        o_ref[:, c * HEAD_DIM:(c + 1) * HEAD_DIM] = r.astype(BF16)


def _k_proj(xn, w, layer, ra, rb, *, seq, tm=1024, tn=1024):
    t, d = xn.shape
    seq_tiles = seq // tm
    k_tiles = QK_WIDTH // tn
    return pl.pallas_call(
        functools.partial(_k_kernel, tn=tn),
        grid=(k_tiles, t // tm),
        in_specs=[
            pl.BlockSpec((tm, d), lambda j, i: (i, 0)),
            pl.BlockSpec((None, d, tn), lambda j, i: (layer, 0, k_tiles + j)),
            pl.BlockSpec((tm, HEAD_DIM), lambda j, i: (i % seq_tiles, 0)),
            pl.BlockSpec((tm, HEAD_DIM), lambda j, i: (i % seq_tiles, 0)),
        ],
        out_specs=pl.BlockSpec((tm, tn), lambda j, i: (i, j)),
        out_shape=jax.ShapeDtypeStruct((t, QK_WIDTH), BF16),
        scratch_shapes=[pltpu.VMEM((d, tn), BF16)],
        compiler_params=_params(("parallel", "arbitrary")),
        name="k_proj",
    )(xn, w, ra, rb)


def _qvt_kernel(xn_ref, w_ref, rat_ref, rbt_ref, o_ref, wt_bf_ref, *, tn):
    j = pl.program_id(0)
    n_q_tiles = QK_WIDTH // tn

    @pl.when(pl.program_id(1) == 0)
    def _():
        wt_bf_ref[...] = w_ref[...].T.astype(BF16)

    acc = lax.dot_general(wt_bf_ref[...], xn_ref[...], (((1,), (1,)), ((), ())),
                          preferred_element_type=F32)

    @pl.when(j < n_q_tiles)
    def _():
        scale = LOG2_E * HEAD_DIM ** -0.5
        rat = rat_ref[...] * scale
        rbt = rbt_ref[...] * scale
        half = HEAD_DIM // 2
        for c in range(tn // HEAD_DIM):
            t = acc[c * HEAD_DIM:(c + 1) * HEAD_DIM, :]
            swapped = jnp.concatenate([t[half:, :], t[:half, :]], axis=0)
            o_ref[c * HEAD_DIM:(c + 1) * HEAD_DIM, :] = (t * rat + swapped * rbt).astype(BF16)

    @pl.when(j >= n_q_tiles)
    def _():
        o_ref[...] = acc.astype(BF16)


def _qvt_proj(xn, w, layer, rat, rbt, *, seq, tm=1024, tn=1024):
    t, d = xn.shape
    n = 2 * QK_WIDTH
    seq_tiles = seq // tm
    n_q_tiles = QK_WIDTH // tn
    w_col = lambda j: jnp.where(j < n_q_tiles, j, j + n_q_tiles)
    return pl.pallas_call(
        functools.partial(_qvt_kernel, tn=tn),
        grid=(n // tn, t // tm),
        in_specs=[
            pl.BlockSpec((tm, d), lambda j, i: (i, 0)),
            pl.BlockSpec((None, d, tn), lambda j, i: (layer, 0, w_col(j))),
            pl.BlockSpec((HEAD_DIM, tm), lambda j, i: (0, i % seq_tiles)),
            pl.BlockSpec((HEAD_DIM, tm), lambda j, i: (0, i % seq_tiles)),
        ],
        out_specs=pl.BlockSpec((tn, tm), lambda j, i: (j, i)),
        out_shape=jax.ShapeDtypeStruct((n, t), BF16),
        scratch_shapes=[pltpu.VMEM((tn, d), BF16)],
        compiler_params=_params(("parallel", "arbitrary")),
        name="qvt_proj",
    )(xn, w, rat, rbt)


def _attn_kernel(*refs, q_tiles, **tile_params):
    acc_ref = refs[-1]
    acc_ref[1] = jnp.zeros(acc_ref.shape[1:], F32)
    _attn_tile(0, *refs, q_tiles=q_tiles, first_scores_only=True, **tile_params)

    def tile(qi, carry):
        _attn_tile(qi, *refs, q_tiles=q_tiles, **tile_params)
        return carry

    lax.fori_loop(0, q_tiles, tile, 0)


def _attn_tile(qi, qt_ref, k_ref, vt_ref, lq1_ref, lk1_ref, lq2_ref, lk2_ref, sg_ref,
               o_ref, s0_ref, s1_ref, c0_ref, c1_ref, p0_ref, p1_ref, a0_ref, a1_ref,
               m_ref, l_ref, acc_ref, *, tq, tk, q_tiles, heads, lambda_init,
               first_scores_only=False):
    q_start = pl.multiple_of(qi * tq, tq)
    units = [(hh, c) for hh in range(heads) for c in range(2)]
    s_refs = (s0_ref, s1_ref)
    p_refs, alpha_refs = (p0_ref, p1_ref), (a0_ref, a1_ref)
    chunk_max_refs = (c0_ref, c1_ref)

    m_ref[1] = jnp.full(m_ref.shape[1:], MASK_VALUE, F32)
    l_ref[1] = jnp.zeros(l_ref.shape[1:], F32)

    def kv_start(kj):
        return pl.multiple_of(kj * tk, tk)

    def scores(kj, slot, only=None, tile_start=q_start):
        for u, (hh, c) in enumerate(units):
            if only is not None and hh != only:
                continue
            start = hh * V_HEAD_DIM + c * HEAD_DIM
            part = slice(start, start + HEAD_DIM)
            s = jnp.dot(k_ref[pl.ds(kv_start(kj), tk), part],
                        qt_ref[part, pl.ds(tile_start, tq)],
                        preferred_element_type=F32)
            s_refs[slot][u] = s
            chunk_max_refs[slot][u] = jnp.max(s, axis=0, keepdims=True)

    def softmax(slot, masked, only=None):
        prev = 1 - slot
        for c, (hh, _) in enumerate(units):
            if only is not None and hh != only:
                continue
            s = s_refs[slot][c]
            if masked:
                key = lax.broadcasted_iota(jnp.int32, (tk, tq), 0)
                query = lax.broadcasted_iota(jnp.int32, (tk, tq), 1)
                s = jnp.where(key <= query, s, MASK_VALUE)
                chunk_max = jnp.max(s, axis=0, keepdims=True)
            else:
                chunk_max = chunk_max_refs[slot][c]
            m_prev = m_ref[prev, c]
            m_new = jnp.maximum(m_prev, chunk_max)
            alpha = jnp.exp2(m_prev - m_new)
            p = jnp.exp2(s - m_new)
            l_ref[slot, c] = alpha * l_ref[prev, c] + jnp.sum(p, axis=0, keepdims=True)
            m_ref[slot, c] = m_new
            alpha_refs[slot][c] = alpha
            p_refs[slot][c] = p.astype(BF16)

    def accumulate(kj, slot, only=None):
        for u, (hh, _) in enumerate(units):
            if only is not None and hh != only:
                continue
            vt = vt_ref[hh * V_HEAD_DIM:(hh + 1) * V_HEAD_DIM, pl.ds(kv_start(kj), tk)]
            pv = jnp.dot(vt, p_refs[slot][u], preferred_element_type=F32)
            acc_ref[slot, u] = alpha_refs[slot][u] * acc_ref[1 - slot, u] + pv

    def pipeline_step(kj, slot, only=None):
        scores(kj + 2, slot, only)
        softmax(1 - slot, masked=False, only=only)
        accumulate(kj, slot, only)

    if first_scores_only:
        scores(0, 0)
        return

    def fill():
        scores(1, 1)
        softmax(0, masked=False)

    _run_if(qi >= 1, fill)

    steady_steps = qi - 1

    def pair(u, carry):
        for hh in range(heads):
            pipeline_step(2 * u, 0, hh)
            pipeline_step(2 * u + 1, 1, hh)
        return carry

    lax.fori_loop(0, steady_steps // 2, pair, 0)

    qi_even = qi % 2 == 0
    qi_odd = jnp.logical_not(qi_even)
    even_with_prev = jnp.logical_and(qi >= 2, qi_even)
    _run_if(even_with_prev, lambda: [pipeline_step(qi - 2, 0, hh) for hh in range(heads)])

    def drain(slot, with_prev=True):
        for hh in range(heads):
            softmax(slot, masked=True, only=hh)
            if with_prev:
                accumulate(qi - 1, 1 - slot, hh)
            accumulate(qi, slot, hh)

    _run_if(even_with_prev, lambda: drain(0))
    _run_if(qi_odd, lambda: drain(1))
    _run_if(qi == 0, lambda: drain(0, with_prev=False))

    lam = (jnp.exp(jnp.sum(lq1_ref[...] * lk1_ref[...], axis=-1, keepdims=True))
           - jnp.exp(jnp.sum(lq2_ref[...] * lk2_ref[...], axis=-1, keepdims=True))
           + lambda_init)
    gain = sg_ref[...] * (1.0 - lambda_init)
    last = qi % 2
    for hh in range(heads):
        u1, u2 = 2 * hh, 2 * hh + 1
        ot = (acc_ref[last, u1] / l_ref[last, u1]
              - lam * (acc_ref[last, u2] / l_ref[last, u2]))
        ms = jnp.mean(ot * ot, axis=0, keepdims=True)
        ot = ot * lax.rsqrt(ms + SUBLN_EPS)
        o_ref[pl.ds(q_start, tq), hh * V_HEAD_DIM:(hh + 1) * V_HEAD_DIM] = (
            ot.T * gain).astype(BF16)
    next_tile = jnp.minimum(qi + 1, q_tiles - 1)
    scores(0, 0, tile_start=pl.multiple_of(next_tile * tq, tq))


def _diff_attention(qvt, k, lq1, lk1, lq2, lk2, subln_g, *, batch, seq, lambda_init,
                    tq=512, heads=2):
    t = k.shape[0]
    tk = tq
    q_tiles = seq // tq
    head_groups = N_DIFF_HEADS // heads
    width = heads * V_HEAD_DIM
    units = 2 * heads
    vec = lambda a: a.reshape(1, -1)
    small = lambda width: pl.BlockSpec((1, width), lambda b, h: (0, 0))
    return pl.pallas_call(
        functools.partial(_attn_kernel, tq=tq, tk=tk, q_tiles=q_tiles, heads=heads,
                          lambda_init=lambda_init),
        grid=(batch, head_groups),
        in_specs=[
            pl.BlockSpec((width, seq), lambda b, h: (h, b)),
            pl.BlockSpec((seq, width), lambda b, h: (b, h)),
            pl.BlockSpec((width, seq), lambda b, h: (head_groups + h, b)),
            small(HEAD_DIM), small(HEAD_DIM), small(HEAD_DIM), small(HEAD_DIM),
            small(V_HEAD_DIM),
        ],
        out_specs=pl.BlockSpec((seq, width), lambda b, h: (b, h)),
        out_shape=jax.ShapeDtypeStruct((t, N_DIFF_HEADS * V_HEAD_DIM), BF16),
        scratch_shapes=[
            pltpu.VMEM((units, tk, tq), F32), pltpu.VMEM((units, tk, tq), F32),
            pltpu.VMEM((units, 1, tq), F32), pltpu.VMEM((units, 1, tq), F32),
            pltpu.VMEM((units, tk, tq), BF16), pltpu.VMEM((units, tk, tq), BF16),
            pltpu.VMEM((units, 1, tq), F32), pltpu.VMEM((units, 1, tq), F32),
            pltpu.VMEM((2, units, 1, tq), F32),
            pltpu.VMEM((2, units, 1, tq), F32),
            pltpu.VMEM((2, units, V_HEAD_DIM, tq), F32),
        ],
        compiler_params=_params(("parallel", "parallel")),
        name="diff_attention",
    )(qvt, k, qvt, vec(lq1), vec(lk1), vec(lq2), vec(lk2), vec(subln_g))


def _emit_residual(h, g_ref, out_refs, final):
    if final:
        out_refs[0][...] = _rms_norm_rows(h, g_ref[...], RMS_EPS)
    else:
        out_refs[0][...] = h
        out_refs[1][...] = _rms_norm_rows(h, g_ref[...], RMS_EPS).astype(BF16)


def _residual_out(t, d, tm, final):
    spec = pl.BlockSpec((tm, d), lambda i: (i, 0))
    if final:
        return [spec], [jax.ShapeDtypeStruct((t, d), F32)]
    return [spec, spec], [jax.ShapeDtypeStruct((t, d), F32), jax.ShapeDtypeStruct((t, d), BF16)]


def _resident_weight(w_ref, scratch_refs):
    if not scratch_refs:
        return w_ref[...]
    _cast_weights_once([(w_ref, scratch_refs[0])], row_axis=0)
    return scratch_refs[0][...]


def _weight_scratch(w):
    return [pltpu.VMEM(w.shape[1:], BF16)] if w.dtype == F32 else []


def _proj_residual_kernel(a_ref, w_ref, r_ref, g_ref, *rest, final):
    n_out = 1 if final else 2
    w = _resident_weight(w_ref, rest[n_out:])
    h = r_ref[...] + jnp.dot(a_ref[...], w, preferred_element_type=F32)
    _emit_residual(h, g_ref, rest[:n_out], final)


def _proj_residual(a, w, layer, res, g_next, *, final=False, tm=512):
    t, k = a.shape
    n = w.shape[2]
    out_specs, out_shape = _residual_out(t, n, tm, final)
    return pl.pallas_call(
        functools.partial(_proj_residual_kernel, final=final),
        grid=(t // tm,),
        scratch_shapes=_weight_scratch(w),
        in_specs=[
            pl.BlockSpec((tm, k), lambda i: (i, 0)),
            pl.BlockSpec((None, k, n), lambda i: (layer, 0, 0), pipeline_mode=pl.Buffered(1)),
            pl.BlockSpec((tm, n), lambda i: (i, 0)),
            pl.BlockSpec((1, n), lambda i: (0, 0)),
        ],
        out_specs=out_specs,
        out_shape=out_shape,
        compiler_params=_params(("arbitrary",)),
        name="proj_residual",
    )(a, w, res, g_next)


def _bch_kernel(xn_ref, wb_ref, wc_ref, wu_ref, gb_ref, z_ref, wb_bf_ref, wc_bf_ref,
                wu_bf_ref):
    _cast_weights_once([(wb_ref, wb_bf_ref), (wc_ref, wc_bf_ref), (wu_ref, wu_bf_ref)])
    xn = xn_ref[...]
    gb_ref[...] = jnp.dot(xn, wb_bf_ref[...], preferred_element_type=F32).astype(BF16)
    gate_c = jnp.dot(xn, wc_bf_ref[...], preferred_element_type=F32)
    u = jnp.dot(xn, wu_bf_ref[...], preferred_element_type=F32)
    z_ref[...] = (gate_c * u).astype(BF16)


def _bch_proj(xn, w, layer, *, tm=1024, tn=512):
    t, d = xn.shape
    col_tiles = d // tn
    w_spec = lambda part: pl.BlockSpec((None, d, tn),
                                       lambda j, i: (layer, 0, part * col_tiles + j))
    out_spec = pl.BlockSpec((tm, tn), lambda j, i: (i, j))
    return pl.pallas_call(
        _bch_kernel,
        grid=(col_tiles, t // tm),
        in_specs=[pl.BlockSpec((tm, d), lambda j, i: (i, 0)), w_spec(0), w_spec(1), w_spec(2)],
        out_specs=[out_spec, out_spec],
        out_shape=[jax.ShapeDtypeStruct((t, d), BF16)] * 2,
        scratch_shapes=[pltpu.VMEM((d, tn), BF16)] * 3,
        compiler_params=_params(("parallel", "arbitrary")),
        name="bch_proj",
    )(xn, w, w, w)


def _conv_out_kernel(gb_ref, z_ref, halo_ref, cw_ref, w_ref, r_ref, g_ref, *rest,
                     seq_tiles, final):
    n_out = 1 if final else 2
    w = _resident_weight(w_ref, rest[n_out:])
    tm = z_ref.shape[0]
    z = z_ref[...].astype(F32)
    halo = jnp.where(pl.program_id(0) % seq_tiles == 0, 0.0, halo_ref[...].astype(F32))
    zext = jnp.concatenate([halo, z], axis=0)
    pad = halo.shape[0]
    cw = cw_ref[...]
    zc = cw[2:3, :] * z
    zc = zc + cw[1:2, :] * zext[pad - 1:pad - 1 + tm, :]
    zc = zc + cw[0:1, :] * zext[pad - 2:pad - 2 + tm, :]
    a = (gb_ref[...].astype(F32) * zc).astype(BF16)
    h = r_ref[...] + jnp.dot(a, w, preferred_element_type=F32)
    _emit_residual(h, g_ref, rest[:n_out], final)


def _conv_out_proj(gate_b, z, conv_w, w, layer, res, g_next, *, seq, final=False, tm=512):
    t, d = z.shape
    n = w.shape[2]
    halo_rows = 2 * V7X_F32_SUBLANES
    halo_per_tile = tm // halo_rows
    row_spec = pl.BlockSpec((tm, d), lambda i: (i, 0))
    out_specs, out_shape = _residual_out(t, n, tm, final)
    return pl.pallas_call(
        functools.partial(_conv_out_kernel, seq_tiles=seq // tm, final=final),
        grid=(t // tm,),
        in_specs=[
            row_spec,
            row_spec,
            pl.BlockSpec((halo_rows, d), lambda i: (jnp.maximum(i * halo_per_tile - 1, 0), 0)),
            pl.BlockSpec((None, CONV_WIDTH, d), lambda i: (layer, 0, 0)),
            pl.BlockSpec((None, d, n), lambda i: (layer, 0, 0), pipeline_mode=pl.Buffered(1)),
            pl.BlockSpec((tm, n), lambda i: (i, 0)),
            pl.BlockSpec((1, n), lambda i: (0, 0)),
        ],
        out_specs=out_specs,
        out_shape=out_shape,
        scratch_shapes=_weight_scratch(w),
        compiler_params=_params(("arbitrary",)),
        name="conv_out_proj",
    )(gate_b, z, z, conv_w, w, res, g_next)


def _ffn_in_kernel(xn_ref, wg_ref, wu_ref, o_ref, wg_bf_ref, wu_bf_ref):
    _cast_weights_once([(wg_ref, wg_bf_ref), (wu_ref, wu_bf_ref)])
    xn = xn_ref[...]
    gate = jnp.dot(xn, wg_bf_ref[...], preferred_element_type=F32)
    up = jnp.dot(xn, wu_bf_ref[...], preferred_element_type=F32)
    o_ref[...] = (gate * jax.nn.sigmoid(gate) * up).astype(BF16)


def _ffn_in(xn, w_gate, w_up, layer, *, tm=1024, tf=512):
    t, d = xn.shape
    f = w_gate.shape[2]
    w_spec = pl.BlockSpec((None, d, tf), lambda j, i: (layer, 0, j))
    return pl.pallas_call(
        _ffn_in_kernel,
        grid=(f // tf, t // tm),
        in_specs=[pl.BlockSpec((tm, d), lambda j, i: (i, 0)), w_spec, w_spec],
        out_specs=pl.BlockSpec((tm, tf), lambda j, i: (i, j)),
        out_shape=jax.ShapeDtypeStruct((t, f), BF16),
        scratch_shapes=[pltpu.VMEM((d, tf), BF16)] * 2,
        compiler_params=_params(("parallel", "arbitrary")),
        name="ffn_in",
    )(xn, w_gate, w_up)


def _first_norm_kernel(x_ref, g_ref, o_ref):
    o_ref[...] = _rms_norm_rows(x_ref[...], g_ref[...], RMS_EPS).astype(BF16)


def _first_norm(h, g, *, tm=1024):
    t, d = h.shape
    return pl.pallas_call(
        _first_norm_kernel,
        grid=(t // tm,),
        in_specs=[pl.BlockSpec((tm, d), lambda i: (i, 0)),
                  pl.BlockSpec((1, d), lambda i: (0, 0))],
        out_specs=pl.BlockSpec((tm, d), lambda i: (i, 0)),
        out_shape=jax.ShapeDtypeStruct((t, d), BF16),
        compiler_params=_params(("parallel",)),
        name="first_norm",
    )(h, g)


def _rope_tables(seq):
    pos = jnp.arange(seq, dtype=F32)
    inv_freq = 1.0 / (ROPE_THETA ** (jnp.arange(0, HEAD_DIM, 2, dtype=F32) / HEAD_DIM))
    ang = pos[:, None] * inv_freq[None, :]
    cos, sin = jnp.cos(ang), jnp.sin(ang)
    return (jnp.concatenate([cos, cos], axis=-1),
            jnp.concatenate([-sin, sin], axis=-1))


def kernel(x, attn_norm_g, w_qkv, w_o_attn, lambda_q1, lambda_k1, lambda_q2, lambda_k2,
           subln_g, conv_norm_g, w_bch, conv_w, w_o_conv, ffn_norm_g, w_gate, w_up,
           w_down, final_norm_g):
    batch, seq, d = x.shape
    rope_a, rope_b = _rope_tables(seq)
    rope_a_t, rope_b_t = rope_a.T, rope_b.T
    row = lambda a: a.reshape(1, -1)
    w_down_b = w_down.astype(BF16)

    def mixer_norm_g(layer):
        j = layer // 2
        return row(attn_norm_g[j] if layer % 2 == 0 else conv_norm_g[j])

    h = x.reshape(batch * seq, d)
    xn = _first_norm(h, mixer_norm_g(0))
    for i in range(DEPTH):
        j = i // 2
        g_ffn = row(ffn_norm_g[i])
        if i % 2 == 0:
            lambda_init = 0.8 - 0.6 * math.exp(-0.3 * i)
            k = _k_proj(xn, w_qkv, j, rope_a, rope_b, seq=seq)
            qvt = _qvt_proj(xn, w_qkv, j, rope_a_t, rope_b_t, seq=seq)
            o = _diff_attention(qvt, k, lambda_q1[j], lambda_k1[j], lambda_q2[j],
                                lambda_k2[j], subln_g[j], batch=batch, seq=seq,
                                lambda_init=lambda_init)
            h, xn = _proj_residual(o, w_o_attn, j, h, g_ffn)
        else:
            gate_b, z = _bch_proj(xn, w_bch, j)
            h, xn = _conv_out_proj(gate_b, z, conv_w, w_o_conv, j, h, g_ffn, seq=seq)
        hid = _ffn_in(xn, w_gate, w_up, i)
        if i + 1 < DEPTH:
            h, xn = _proj_residual(hid, w_down_b, i, h, mixer_norm_g(i + 1))
        else:
            (out,) = _proj_residual(hid, w_down_b, i, h, row(final_norm_g), final=True)
    return out.reshape(batch, seq, d)
```

```python
import functools
import math

import jax
import jax.numpy as jnp
from jax import lax
from jax.experimental import pallas as pl
from jax.experimental.pallas import tpu as pltpu

DEPTH = 4
N_DIFF_HEADS = 8
HEAD_DIM = 128
V_HEAD_DIM = 2 * HEAD_DIM
QK_WIDTH = 2 * N_DIFF_HEADS * HEAD_DIM
ROPE_THETA = 10000.0
CONV_WIDTH = 3
RMS_EPS = 1e-6
SUBLN_EPS = 1e-5

V7X_F32_SUBLANES = 8
V7X_VMEM_BYTES = 64 * 1024 * 1024
V7X_VMEM_RESERVE_BYTES = 6 * 1024 * 1024
VMEM_LIMIT_BYTES = V7X_VMEM_BYTES - V7X_VMEM_RESERVE_BYTES

PROJ_ROW_TILE = 1024
PROJ_COL_TILE = 1024
GATED_COL_TILE = 512
RESIDUAL_ROW_TILE = 512
ATTN_TILE = 512

MASK_VALUE = -1e30
LOG2_E = math.log2(math.e)

F32 = jnp.float32
BF16 = jnp.bfloat16


def _params(semantics):
    return pltpu.CompilerParams(dimension_semantics=semantics,
                                vmem_limit_bytes=VMEM_LIMIT_BYTES)


def _run_if(cond, fn):
    def body(_, carry):
        fn()
        return carry
    lax.fori_loop(0, jnp.asarray(cond, jnp.int32), body, 0)


def _cast_weights_once(pairs, row_axis=1):
    @pl.when(pl.program_id(row_axis) == 0)
    def _():
        for src_ref, dst_ref in pairs:
            dst_ref[...] = src_ref[...].astype(BF16)


def _rms_norm_rows(x, g, eps):
    ms = jnp.mean(x * x, axis=-1, keepdims=True)
    return x * lax.rsqrt(ms + eps) * g


def _k_kernel(xn_ref, w_ref, ra_ref, rb_ref, o_ref, w_bf_ref, *, tn):
    _cast_weights_once([(w_ref, w_bf_ref)])
    acc = jnp.dot(xn_ref[...], w_bf_ref[...], preferred_element_type=F32)
    ra = ra_ref[...]
    rb = rb_ref[...]
    for c in range(tn // HEAD_DIM):
        t = acc[:, c * HEAD_DIM:(c + 1) * HEAD_DIM]
        r = t * ra + pltpu.roll(t, HEAD_DIM // 2, axis=1) * rb
        o_ref[:, c * HEAD_DIM:(c + 1) * HEAD_DIM] = r.astype(BF16)


def _k_proj(xn, w, layer, ra, rb, *, seq, tm=PROJ_ROW_TILE, tn=PROJ_COL_TILE):
    t, d = xn.shape
    seq_tiles = seq // tm
    k_tiles = QK_WIDTH // tn
    return pl.pallas_call(
        functools.partial(_k_kernel, tn=tn),
        grid=(k_tiles, t // tm),
        in_specs=[
            pl.BlockSpec((tm, d), lambda j, i: (i, 0)),
            pl.BlockSpec((None, d, tn), lambda j, i: (layer, 0, k_tiles + j)),
            pl.BlockSpec((tm, HEAD_DIM), lambda j, i: (i % seq_tiles, 0)),
            pl.BlockSpec((tm, HEAD_DIM), lambda j, i: (i % seq_tiles, 0)),
        ],
        out_specs=pl.BlockSpec((tm, tn), lambda j, i: (i, j)),
        out_shape=jax.ShapeDtypeStruct((t, QK_WIDTH), BF16),
        scratch_shapes=[pltpu.VMEM((d, tn), BF16)],
        compiler_params=_params(("parallel", "arbitrary")),
        name="k_proj",
    )(xn, w, ra, rb)


def _qvt_kernel(xn_ref, w_ref, rat_ref, rbt_ref, o_ref, wt_bf_ref, *, tn):
    j = pl.program_id(0)
    n_q_tiles = QK_WIDTH // tn

    @pl.when(pl.program_id(1) == 0)
    def _():
        wt_bf_ref[...] = w_ref[...].T.astype(BF16)

    acc = lax.dot_general(wt_bf_ref[...], xn_ref[...], (((1,), (1,)), ((), ())),
                          preferred_element_type=F32)

    @pl.when(j < n_q_tiles)
    def _():
        scale = LOG2_E * HEAD_DIM ** -0.5
        rat = rat_ref[...] * scale
        rbt = rbt_ref[...] * scale
        half = HEAD_DIM // 2
        for c in range(tn // HEAD_DIM):
            t = acc[c * HEAD_DIM:(c + 1) * HEAD_DIM, :]
            swapped = jnp.concatenate([t[half:, :], t[:half, :]], axis=0)
            o_ref[c * HEAD_DIM:(c + 1) * HEAD_DIM, :] = (t * rat + swapped * rbt).astype(BF16)

    @pl.when(j >= n_q_tiles)
    def _():
        o_ref[...] = acc.astype(BF16)


def _qvt_proj(xn, w, layer, rat, rbt, *, seq, tm=PROJ_ROW_TILE, tn=PROJ_COL_TILE):
    t, d = xn.shape
    n = 2 * QK_WIDTH
    seq_tiles = seq // tm
    n_q_tiles = QK_WIDTH // tn
    w_col = lambda j: jnp.where(j < n_q_tiles, j, j + n_q_tiles)
    return pl.pallas_call(
        functools.partial(_qvt_kernel, tn=tn),
        grid=(n // tn, t // tm),
        in_specs=[
            pl.BlockSpec((tm, d), lambda j, i: (i, 0)),
            pl.BlockSpec((None, d, tn), lambda j, i: (layer, 0, w_col(j))),
            pl.BlockSpec((HEAD_DIM, tm), lambda j, i: (0, i % seq_tiles)),
            pl.BlockSpec((HEAD_DIM, tm), lambda j, i: (0, i % seq_tiles)),
        ],
        out_specs=pl.BlockSpec((tn, tm), lambda j, i: (j, i)),
        out_shape=jax.ShapeDtypeStruct((n, t), BF16),
        scratch_shapes=[pltpu.VMEM((tn, d), BF16)],
        compiler_params=_params(("parallel", "arbitrary")),
        name="qvt_proj",
    )(xn, w, rat, rbt)


def _attn_kernel(*refs, q_tiles, **tile_params):
    acc_ref = refs[-1]
    acc_ref[1] = jnp.zeros(acc_ref.shape[1:], F32)
    _attn_tile(0, *refs, q_tiles=q_tiles, first_scores_only=True, **tile_params)

    def tile(qi, carry):
        _attn_tile(qi, *refs, q_tiles=q_tiles, **tile_params)
        return carry

    lax.fori_loop(0, q_tiles, tile, 0)


def _attn_tile(qi, qt_ref, k_ref, vt_ref, lq1_ref, lk1_ref, lq2_ref, lk2_ref, sg_ref,
               o_ref, s0_ref, s1_ref, c0_ref, c1_ref, p0_ref, p1_ref, a0_ref, a1_ref,
               m_ref, l_ref, acc_ref, *, tq, tk, q_tiles, heads, lambda_init,
               first_scores_only=False):
    q_start = pl.multiple_of(qi * tq, tq)
    units = [(hh, c) for hh in range(heads) for c in range(2)]
    s_refs = (s0_ref, s1_ref)
    p_refs, alpha_refs = (p0_ref, p1_ref), (a0_ref, a1_ref)
    chunk_max_refs = (c0_ref, c1_ref)

    m_ref[1] = jnp.full(m_ref.shape[1:], MASK_VALUE, F32)
    l_ref[1] = jnp.zeros(l_ref.shape[1:], F32)

    def kv_start(kj):
        return pl.multiple_of(kj * tk, tk)

    def scores(kj, slot, only=None, tile_start=q_start):
        for u, (hh, c) in enumerate(units):
            if only is not None and hh != only:
                continue
            start = hh * V_HEAD_DIM + c * HEAD_DIM
            part = slice(start, start + HEAD_DIM)
            s = jnp.dot(k_ref[pl.ds(kv_start(kj), tk), part],
                        qt_ref[part, pl.ds(tile_start, tq)],
                        preferred_element_type=F32)
            s_refs[slot][u] = s
            chunk_max_refs[slot][u] = jnp.max(s, axis=0, keepdims=True)

    def softmax(slot, masked, only=None):
        prev = 1 - slot
        for c, (hh, _) in enumerate(units):
            if only is not None and hh != only:
                continue
            s = s_refs[slot][c]
            if masked:
                key = lax.broadcasted_iota(jnp.int32, (tk, tq), 0)
                query = lax.broadcasted_iota(jnp.int32, (tk, tq), 1)
                s = jnp.where(key <= query, s, MASK_VALUE)
                chunk_max = jnp.max(s, axis=0, keepdims=True)
            else:
                chunk_max = chunk_max_refs[slot][c]
            m_prev = m_ref[prev, c]
            m_new = jnp.maximum(m_prev, chunk_max)
            alpha = jnp.exp2(m_prev - m_new)
            p = jnp.exp2(s - m_new)
            l_ref[slot, c] = alpha * l_ref[prev, c] + jnp.sum(p, axis=0, keepdims=True)
            m_ref[slot, c] = m_new
            alpha_refs[slot][c] = alpha
            p_refs[slot][c] = p.astype(BF16)

    def accumulate(kj, slot, only=None):
        for u, (hh, _) in enumerate(units):
            if only is not None and hh != only:
                continue
            vt = vt_ref[hh * V_HEAD_DIM:(hh + 1) * V_HEAD_DIM, pl.ds(kv_start(kj), tk)]
            pv = jnp.dot(vt, p_refs[slot][u], preferred_element_type=F32)
            acc_ref[slot, u] = alpha_refs[slot][u] * acc_ref[1 - slot, u] + pv

    def pipeline_step(kj, slot, only=None):
        scores(kj + 2, slot, only)
        softmax(1 - slot, masked=False, only=only)
        accumulate(kj, slot, only)

    if first_scores_only:
        scores(0, 0)
        return

    def fill():
        scores(1, 1)
        softmax(0, masked=False)

    _run_if(qi >= 1, fill)

    steady_steps = qi - 1

    def pair(u, carry):
        for hh in range(heads):
            pipeline_step(2 * u, 0, hh)
            pipeline_step(2 * u + 1, 1, hh)
        return carry

    lax.fori_loop(0, steady_steps // 2, pair, 0)

    qi_even = qi % 2 == 0
    qi_odd = jnp.logical_not(qi_even)
    even_with_prev = jnp.logical_and(qi >= 2, qi_even)
    _run_if(even_with_prev, lambda: [pipeline_step(qi - 2, 0, hh) for hh in range(heads)])

    def drain(slot, with_prev=True):
        for hh in range(heads):
            softmax(slot, masked=True, only=hh)
            if with_prev:
                accumulate(qi - 1, 1 - slot, hh)
            accumulate(qi, slot, hh)

    _run_if(even_with_prev, lambda: drain(0))
    _run_if(qi_odd, lambda: drain(1))
    _run_if(qi == 0, lambda: drain(0, with_prev=False))

    lam = (jnp.exp(jnp.sum(lq1_ref[...] * lk1_ref[...], axis=-1, keepdims=True))
           - jnp.exp(jnp.sum(lq2_ref[...] * lk2_ref[...], axis=-1, keepdims=True))
           + lambda_init)
    gain = sg_ref[...] * (1.0 - lambda_init)
    last = qi % 2
    for hh in range(heads):
        u1, u2 = 2 * hh, 2 * hh + 1
        ot = (acc_ref[last, u1] / l_ref[last, u1]
              - lam * (acc_ref[last, u2] / l_ref[last, u2]))
        ms = jnp.mean(ot * ot, axis=0, keepdims=True)
        ot = ot * lax.rsqrt(ms + SUBLN_EPS)
        o_ref[pl.ds(q_start, tq), hh * V_HEAD_DIM:(hh + 1) * V_HEAD_DIM] = (
            ot.T * gain).astype(BF16)
    next_tile = jnp.minimum(qi + 1, q_tiles - 1)
    scores(0, 0, tile_start=pl.multiple_of(next_tile * tq, tq))


def _diff_attention(qvt, k, lq1, lk1, lq2, lk2, subln_g, *, batch, seq, lambda_init,
                    tq=ATTN_TILE, heads=2):
    t = k.shape[0]
    tk = tq
    q_tiles = seq // tq
    head_groups = N_DIFF_HEADS // heads
    width = heads * V_HEAD_DIM
    units = 2 * heads
    vec = lambda a: a.reshape(1, -1)
    small = lambda width: pl.BlockSpec((1, width), lambda b, h: (0, 0))
    return pl.pallas_call(
        functools.partial(_attn_kernel, tq=tq, tk=tk, q_tiles=q_tiles, heads=heads,
                          lambda_init=lambda_init),
        grid=(batch, head_groups),
        in_specs=[
            pl.BlockSpec((width, seq), lambda b, h: (h, b)),
            pl.BlockSpec((seq, width), lambda b, h: (b, h)),
            pl.BlockSpec((width, seq), lambda b, h: (head_groups + h, b)),
            small(HEAD_DIM), small(HEAD_DIM), small(HEAD_DIM), small(HEAD_DIM),
            small(V_HEAD_DIM),
        ],
        out_specs=pl.BlockSpec((seq, width), lambda b, h: (b, h)),
        out_shape=jax.ShapeDtypeStruct((t, N_DIFF_HEADS * V_HEAD_DIM), BF16),
        scratch_shapes=[
            pltpu.VMEM((units, tk, tq), F32), pltpu.VMEM((units, tk, tq), F32),
            pltpu.VMEM((units, 1, tq), F32), pltpu.VMEM((units, 1, tq), F32),
            pltpu.VMEM((units, tk, tq), BF16), pltpu.VMEM((units, tk, tq), BF16),
            pltpu.VMEM((units, 1, tq), F32), pltpu.VMEM((units, 1, tq), F32),
            pltpu.VMEM((2, units, 1, tq), F32),
            pltpu.VMEM((2, units, 1, tq), F32),
            pltpu.VMEM((2, units, V_HEAD_DIM, tq), F32),
        ],
        compiler_params=_params(("parallel", "parallel")),
        name="diff_attention",
    )(qvt, k, qvt, vec(lq1), vec(lk1), vec(lq2), vec(lk2), vec(subln_g))


def _emit_residual(h, g_ref, out_refs, final):
    if final:
        out_refs[0][...] = _rms_norm_rows(h, g_ref[...], RMS_EPS)
    else:
        out_refs[0][...] = h
        out_refs[1][...] = _rms_norm_rows(h, g_ref[...], RMS_EPS).astype(BF16)


def _residual_out(t, d, tm, final):
    spec = pl.BlockSpec((tm, d), lambda i: (i, 0))
    if final:
        return [spec], [jax.ShapeDtypeStruct((t, d), F32)]
    return [spec, spec], [jax.ShapeDtypeStruct((t, d), F32), jax.ShapeDtypeStruct((t, d), BF16)]


def _resident_weight(w_ref, scratch_refs):
    if not scratch_refs:
        return w_ref[...]
    _cast_weights_once([(w_ref, scratch_refs[0])], row_axis=0)
    return scratch_refs[0][...]


def _weight_scratch(w):
    return [pltpu.VMEM(w.shape[1:], BF16)] if w.dtype == F32 else []


def _proj_residual_kernel(a_ref, w_ref, r_ref, g_ref, *rest, final):
    n_out = 1 if final else 2
    w = _resident_weight(w_ref, rest[n_out:])
    h = r_ref[...] + jnp.dot(a_ref[...], w, preferred_element_type=F32)
    _emit_residual(h, g_ref, rest[:n_out], final)


def _proj_residual(a, w, layer, res, g_next, *, final=False, tm=RESIDUAL_ROW_TILE):
    t, k = a.shape
    n = w.shape[2]
    out_specs, out_shape = _residual_out(t, n, tm, final)
    return pl.pallas_call(
        functools.partial(_proj_residual_kernel, final=final),
        grid=(t // tm,),
        scratch_shapes=_weight_scratch(w),
        in_specs=[
            pl.BlockSpec((tm, k), lambda i: (i, 0)),
            pl.BlockSpec((None, k, n), lambda i: (layer, 0, 0), pipeline_mode=pl.Buffered(1)),
            pl.BlockSpec((tm, n), lambda i: (i, 0)),
            pl.BlockSpec((1, n), lambda i: (0, 0)),
        ],
        out_specs=out_specs,
        out_shape=out_shape,
        compiler_params=_params(("arbitrary",)),
        name="proj_residual",
    )(a, w, res, g_next)


def _conv_in_kernel(xn_ref, wb_ref, wc_ref, wu_ref, cw_ref, a_ref, wb_bf_ref, wc_bf_ref,
                    wu_bf_ref, halo_ref, *, seq_tiles):
    _cast_weights_once([(wb_ref, wb_bf_ref), (wc_ref, wc_bf_ref), (wu_ref, wu_bf_ref)])
    tm = xn_ref.shape[0]
    xn = xn_ref[...]
    gate_b = jnp.dot(xn, wb_bf_ref[...], preferred_element_type=F32)
    gate_c = jnp.dot(xn, wc_bf_ref[...], preferred_element_type=F32)
    u = jnp.dot(xn, wu_bf_ref[...], preferred_element_type=F32)
    z = gate_c * u

    @pl.when(pl.program_id(1) == 0)
    def _():
        halo_ref[...] = jnp.zeros(halo_ref.shape, F32)

    kept = halo_ref[...]
    halo = jnp.where(pl.program_id(1) % seq_tiles == 0, 0.0, kept)
    pad = halo.shape[0]
    zext = jnp.concatenate([halo, z], axis=0)
    cw = cw_ref[...]
    zc = cw[2:3, :] * z
    zc = zc + cw[1:2, :] * zext[pad - 1:pad - 1 + tm, :]
    zc = zc + cw[0:1, :] * zext[pad - 2:pad - 2 + tm, :]
    a_ref[...] = (gate_b * zc).astype(BF16)
    halo_ref[...] = z[tm - pad:, :] + 0.0 * kept


def _conv_in(xn, w, conv_w, layer, *, seq, tm=PROJ_ROW_TILE, tn=GATED_COL_TILE):
    t, d = xn.shape
    col_tiles = d // tn
    w_spec = lambda part: pl.BlockSpec((None, d, tn),
                                       lambda j, i: (layer, 0, part * col_tiles + j))
    return pl.pallas_call(
        functools.partial(_conv_in_kernel, seq_tiles=seq // tm),
        grid=(col_tiles, t // tm),
        in_specs=[
            pl.BlockSpec((tm, d), lambda j, i: (i, 0)),
            w_spec(0), w_spec(1), w_spec(2),
            pl.BlockSpec((None, CONV_WIDTH, tn), lambda j, i: (layer, 0, j)),
        ],
        out_specs=pl.BlockSpec((tm, tn), lambda j, i: (i, j)),
        out_shape=jax.ShapeDtypeStruct((t, d), BF16),
        scratch_shapes=[pltpu.VMEM((d, tn), BF16)] * 3
        + [pltpu.VMEM((V7X_F32_SUBLANES, tn), F32)],
        compiler_params=_params(("parallel", "arbitrary")),
        name="conv_in",
    )(xn, w, w, w, conv_w)


def _ffn_in_kernel(xn_ref, wg_ref, wu_ref, o_ref, wg_bf_ref, wu_bf_ref):
    _cast_weights_once([(wg_ref, wg_bf_ref), (wu_ref, wu_bf_ref)])
    xn = xn_ref[...]
    gate = jnp.dot(xn, wg_bf_ref[...], preferred_element_type=F32)
    up = jnp.dot(xn, wu_bf_ref[...], preferred_element_type=F32)
    o_ref[...] = (gate * jax.nn.sigmoid(gate) * up).astype(BF16)


def _ffn_in(xn, w_gate, w_up, layer, *, tm=PROJ_ROW_TILE, tf=GATED_COL_TILE):
    t, d = xn.shape
    f = w_gate.shape[2]
    w_spec = pl.BlockSpec((None, d, tf), lambda j, i: (layer, 0, j))
    return pl.pallas_call(
        _ffn_in_kernel,
        grid=(f // tf, t // tm),
        in_specs=[pl.BlockSpec((tm, d), lambda j, i: (i, 0)), w_spec, w_spec],
        out_specs=pl.BlockSpec((tm, tf), lambda j, i: (i, j)),
        out_shape=jax.ShapeDtypeStruct((t, f), BF16),
        scratch_shapes=[pltpu.VMEM((d, tf), BF16)] * 2,
        compiler_params=_params(("parallel", "arbitrary")),
        name="ffn_in",
    )(xn, w_gate, w_up)


def _first_norm_kernel(x_ref, g_ref, o_ref):
    o_ref[...] = _rms_norm_rows(x_ref[...], g_ref[...], RMS_EPS).astype(BF16)


def _first_norm(h, g, *, tm=PROJ_ROW_TILE):
    t, d = h.shape
    return pl.pallas_call(
        _first_norm_kernel,
        grid=(t // tm,),
        in_specs=[pl.BlockSpec((tm, d), lambda i: (i, 0)),
                  pl.BlockSpec((1, d), lambda i: (0, 0))],
        out_specs=pl.BlockSpec((tm, d), lambda i: (i, 0)),
        out_shape=jax.ShapeDtypeStruct((t, d), BF16),
        compiler_params=_params(("parallel",)),
        name="first_norm",
    )(h, g)


def _rope_tables(seq):
    pos = jnp.arange(seq, dtype=F32)
    inv_freq = 1.0 / (ROPE_THETA ** (jnp.arange(0, HEAD_DIM, 2, dtype=F32) / HEAD_DIM))
    ang = pos[:, None] * inv_freq[None, :]
    cos, sin = jnp.cos(ang), jnp.sin(ang)
    return (jnp.concatenate([cos, cos], axis=-1),
            jnp.concatenate([-sin, sin], axis=-1))


def kernel(x, attn_norm_g, w_qkv, w_o_attn, lambda_q1, lambda_k1, lambda_q2, lambda_k2,
           subln_g, conv_norm_g, w_bch, conv_w, w_o_conv, ffn_norm_g, w_gate, w_up,
           w_down, final_norm_g):
    batch, seq, d = x.shape
    assert d == QK_WIDTH and w_qkv.shape[1:] == (d, 3 * QK_WIDTH)
    assert seq % PROJ_ROW_TILE == 0 and seq % ATTN_TILE == 0
    assert w_gate.shape[2] % GATED_COL_TILE == 0
    rope_a, rope_b = _rope_tables(seq)
    rope_a_t, rope_b_t = rope_a.T, rope_b.T
    row = lambda a: a.reshape(1, -1)
    w_down_b = w_down.astype(BF16)

    def mixer_norm_g(layer):
        j = layer // 2
        return row(attn_norm_g[j] if layer % 2 == 0 else conv_norm_g[j])

    h = x.reshape(batch * seq, d)
    xn = _first_norm(h, mixer_norm_g(0))
    for i in range(DEPTH):
        j = i // 2
        g_ffn = row(ffn_norm_g[i])
        if i % 2 == 0:
            lambda_init = 0.8 - 0.6 * math.exp(-0.3 * i)
            k = _k_proj(xn, w_qkv, j, rope_a, rope_b, seq=seq)
            qvt = _qvt_proj(xn, w_qkv, j, rope_a_t, rope_b_t, seq=seq)
            o = _diff_attention(qvt, k, lambda_q1[j], lambda_k1[j], lambda_q2[j],
                                lambda_k2[j], subln_g[j], batch=batch, seq=seq,
                                lambda_init=lambda_init)
            h, xn = _proj_residual(o, w_o_attn, j, h, g_ffn)
        else:
            a = _conv_in(xn, w_bch, conv_w, j, seq=seq)
            h, xn = _proj_residual(a, w_o_conv, j, h, g_ffn)
        hid = _ffn_in(xn, w_gate, w_up, i)
        if i + 1 < DEPTH:
            h, xn = _proj_residual(hid, w_down_b, i, h, mixer_norm_g(i + 1))
        else:
            (out,) = _proj_residual(hid, w_down_b, i, h, row(final_norm_g), final=True)
    return out.reshape(batch, seq, d)
```
